```python
import math
import jax
import jax.numpy as jnp
from jax import lax
import numpy as np

D_MODEL = 1024
BATCH = 8
SEQ = 2048
DEPTH = 4

HEAD_DIM = 64
H_RWKV = 6
H_RET = 4
H_GDN = 6
D_RWKV = H_RWKV * HEAD_DIM
D_RET = H_RET * HEAD_DIM
D_GDN = H_GDN * HEAD_DIM
D_MIX = D_RWKV + D_RET + D_GDN
RWKV_DECAY_RANK = 64
RWKV_ICLR_RANK = 64
RWKV_VRES_RANK = 32
RWKV_GATE_RANK = 128
RWKV_COLS = 3 * D_RWKV + RWKV_DECAY_RANK + RWKV_ICLR_RANK + RWKV_GATE_RANK
RET_COLS = 4 * D_RET
GDN_COLS = 4 * D_GDN + 2 * H_GDN
IN_COLS_FIRST = RWKV_COLS + RET_COLS + GDN_COLS
IN_COLS_REST = IN_COLS_FIRST + RWKV_VRES_RANK
GDN_CONV = 4
RET_CHUNK = 128
GDN_CHUNK = 64
D_FF = 2816
NORM_EPS = 1e-6
L2_EPS = 1e-6
RWKV_GN_EPS = HEAD_DIM * 1e-5
RET_GN_EPS = 1e-5
ROPE_THETA = 10000.0
POS_OFFSET_MAX = 4096

kernel_name = 'hybrid_rwkv7_retnet_gdn_macaron'


def rms_norm(x, g):
    xf = x.astype(jnp.float32)
    y = xf * lax.rsqrt(jnp.mean(xf * xf, axis=-1, keepdims=True) + NORM_EPS)
    return (y * g).astype(x.dtype)


def group_norm_heads(y, eps):
    mu = jnp.mean(y, axis=-1, keepdims=True)
    var = jnp.mean(jnp.square(y - mu), axis=-1, keepdims=True)
    return (y - mu) * lax.rsqrt(var + eps)


def l2norm(x):
    return x * lax.rsqrt(jnp.sum(x * x, axis=-1, keepdims=True) + L2_EPS)


def swiglu(h, w_gate, w_up, w_down):
    return (jax.nn.silu(h @ w_gate) * (h @ w_up)) @ w_down


def token_shift(p):
    return jnp.pad(p, ((0, 0), (1, 0), (0, 0)))[:, :-1]


def causal_depthwise_conv(x, w):
    K, T = w.shape[0], x.shape[1]
    xp = jnp.pad(x, ((0, 0), (K - 1, 0), (0, 0)))
    return sum(xp[:, j:j + T] * w[j] for j in range(K))


def rope(x, cos, sin):
    x1, x2 = jnp.split(x, 2, axis=-1)
    return jnp.concatenate([x1 * cos - x2 * sin, x2 * cos + x1 * sin], axis=-1)


def rwkv7_time_mix(p, vres, v_first, mu, w0, w_up, a0, a_up, g_up, k_k, k_a, r_k, ln_w, ln_b):
    B, T, _ = p.shape
    p = p + (token_shift(p) - p) * mu
    r, k, v, wc, ac, gc = jnp.split(p, [D_RWKV, 2 * D_RWKV, 3 * D_RWKV, 3 * D_RWKV + RWKV_DECAY_RANK,
                                        3 * D_RWKV + RWKV_DECAY_RANK + RWKV_ICLR_RANK], axis=-1)
    log_w = -math.exp(-0.5) * jax.nn.sigmoid(w0 + jnp.tanh(wc) @ w_up)
    a = jax.nn.sigmoid(a0 + ac @ a_up)
    g = jax.nn.sigmoid(gc) @ g_up
    if vres is None:
        v_first = v
    else:
        p_v, mu_v, v0, v_up = vres
        p_v = p_v + (token_shift(p_v) - p_v) * mu_v
        v = v + (v_first - v) * jax.nn.sigmoid(v0 + p_v @ v_up)

    def heads(z):
        return z.reshape(B, T, H_RWKV, HEAD_DIM)

    kk = l2norm(heads(k * k_k))
    k = k * (1.0 + (a - 1.0) * k_a)
    r, k, v, a, w = heads(r), heads(k), heads(v), heads(a), heads(jnp.exp(log_w))

    def step(S, inp):
        r_t, w_t, k_t, v_t, kk_t, a_t = inp
        sa = jnp.einsum('bhvk,bhk->bhv', S, -kk_t)
        S = S * w_t[:, :, None, :] + sa[..., None] * (kk_t * a_t)[:, :, None, :] + v_t[..., None] * k_t[:, :, None, :]
        return S, jnp.einsum('bhvk,bhk->bhv', S, r_t)

    xs = tuple(jnp.moveaxis(z, 1, 0) for z in (r, w, k, v, kk, a))
    S0 = jnp.zeros((B, H_RWKV, HEAD_DIM, HEAD_DIM), jnp.float32)
    _, y = lax.scan(step, S0, xs)
    y = jnp.moveaxis(y, 0, 1)
    y = group_norm_heads(y, RWKV_GN_EPS).reshape(B, T, D_RWKV) * ln_w + ln_b
    bonus = jnp.sum(r * k * r_k.reshape(H_RWKV, HEAD_DIM), axis=-1, keepdims=True) * v
    return (y + bonus.reshape(B, T, D_RWKV)) * g, v_first


def retention(p, cos, sin, gn_w):
    B, T, _ = p.shape
    C = RET_CHUNK
    nc = T // C
    q, k, v, gate = jnp.split(p, 4, axis=-1)
    q = rope(q.reshape(B, T, H_RET, HEAD_DIM), cos, sin) * HEAD_DIM ** -0.5
    k = rope(k.reshape(B, T, H_RET, HEAD_DIM), cos, sin)
    v = v.reshape(B, T, H_RET, HEAD_DIM)
    log_gamma = jnp.log(1.0 - 2.0 ** (-5.0 - jnp.arange(H_RET, dtype=jnp.float32)))

    def chunks(z):
        return z.reshape(B, nc, C, H_RET, HEAD_DIM).transpose(0, 3, 1, 2, 4)

    qc, kc, vc = chunks(q), chunks(k), chunks(v)
    idx = jnp.arange(C)
    rel = idx[:, None] - idx[None, :]
    decay = jnp.where(rel >= 0, jnp.exp(log_gamma[:, None, None] * jnp.maximum(rel, 0)), 0.0)
    scores = jnp.einsum('bhnid,bhnjd->bhnij', qc, kc) * decay[None, :, None]
    o_inner = jnp.einsum('bhnij,bhnjv->bhniv', scores, vc)
    k_w = jnp.exp(log_gamma[:, None] * (C - 1 - idx))
    U = jnp.einsum('bhnjd,hj,bhnjv->nbhdv', kc, k_w, vc)
    chunk_decay = jnp.exp(log_gamma * C)[None, :, None, None]

    def step(R, U_c):
        return R * chunk_decay + U_c, R

    R0 = jnp.zeros((B, H_RET, HEAD_DIM, HEAD_DIM), jnp.float32)
    _, R_prev = lax.scan(step, R0, U)
    q_w = jnp.exp(log_gamma[:, None] * (idx + 1))
    o_cross = jnp.einsum('bhnid,hi,nbhdv->bhniv', qc, q_w, R_prev)
    o = (o_inner + o_cross).transpose(0, 2, 3, 1, 4).reshape(B, T, H_RET, HEAD_DIM)
    o = group_norm_heads(o, RET_GN_EPS).reshape(B, T, D_RET) * gn_w
    return jax.nn.silu(gate) * o


def chunk_gated_delta_rule(q, k, v, g, beta):
    B, T, H, dk = q.shape
    dv = v.shape[-1]
    C = GDN_CHUNK
    nc = T // C

    def chunks(z):
        return jnp.moveaxis(z.reshape((B, nc, C, H) + z.shape[3:]), 3, 1)

    q, k, v, g, beta = chunks(q), chunks(k), chunks(v), chunks(g), chunks(beta)
    gc = jnp.cumsum(g, axis=-1)
    idx = jnp.arange(C)
    tril = idx[:, None] >= idx[None, :]
    strict = idx[:, None] > idx[None, :]
    diff = gc[..., :, None] - gc[..., None, :]
    decay_mask = jnp.where(tril, jnp.exp(jnp.where(tril, diff, 0.0)), 0.0)
    kb = k * beta[..., None]
    vb = v * beta[..., None]
    Lm = jnp.where(strict, jnp.einsum('bhnid,bhnjd->bhnij', kb, k) * decay_mask, 0.0)
    eye = jnp.eye(C, dtype=jnp.float32)
    Tm = lax.linalg.triangular_solve(Lm + eye, jnp.broadcast_to(eye, Lm.shape),
                                     left_side=True, lower=True, unit_diagonal=True)
    u = Tm @ vb
    w = Tm @ (kb * jnp.exp(gc)[..., None])
    attn = jnp.where(tril, jnp.einsum('bhnid,bhnjd->bhnij', q, k) * decay_mask, 0.0)
    q_e = q * jnp.exp(gc)[..., None]
    k_tail = k * jnp.exp(gc[..., -1:] - gc)[..., None]
    g_last = jnp.exp(gc[..., -1])

    def step(S, inp):
        qe_c, A_c, u_c, w_c, kt_c, gl_c = inp
        v_new = u_c - jnp.einsum('bhcd,bhdv->bhcv', w_c, S)
        o_c = jnp.einsum('bhcd,bhdv->bhcv', qe_c, S) + jnp.einsum('bhij,bhjv->bhiv', A_c, v_new)
        S = S * gl_c[..., None, None] + jnp.einsum('bhcd,bhcv->bhdv', kt_c, v_new)
        return S, o_c

    xs = tuple(jnp.moveaxis(z, 2, 0) for z in (q_e, attn, u, w, k_tail, g_last))
    S0 = jnp.zeros((B, H, dk, dv), jnp.float32)
    _, o = lax.scan(step, S0, xs)
    return o.transpose(1, 0, 3, 2, 4).reshape(B, T, H, dv)


def gated_deltanet(p, conv_w, A_log, dt_bias, norm_w):
    B, T, _ = p.shape
    qkv, gate, beta_logit, a_logit = jnp.split(p, [3 * D_GDN, 4 * D_GDN, 4 * D_GDN + H_GDN], axis=-1)
    qkv = jax.nn.silu(causal_depthwise_conv(qkv, conv_w))
    q, k, v = (z.reshape(B, T, H_GDN, HEAD_DIM) for z in jnp.split(qkv, 3, axis=-1))
    q = l2norm(q) * HEAD_DIM ** -0.5
    k = l2norm(k)
    beta = jax.nn.sigmoid(beta_logit)
    g = -jnp.exp(A_log) * jax.nn.softplus(a_logit + dt_bias)
    o = chunk_gated_delta_rule(q, k, v, g, beta)
    o = o * lax.rsqrt(jnp.mean(o * o, axis=-1, keepdims=True) + NORM_EPS) * norm_w
    o = o * jax.nn.silu(gate.reshape(B, T, H_GDN, HEAD_DIM))
    return o.reshape(B, T, D_GDN)


def setup_inputs(seed: int = 0) -> dict:
    key = jax.random.key(seed)
    keys = jax.random.split(key, 32)
    ks = iter([keys[i] for i in range(32)])

    def nrm(shape, scale):
        return scale * jax.random.normal(next(ks), shape, jnp.float32)

    def unif(shape, lo, hi):
        return jax.random.uniform(next(ks), shape, jnp.float32, lo, hi)

    L, Lr = DEPTH, DEPTH - 1
    x = nrm((BATCH, SEQ, D_MODEL), 1.0)
    positions = (jax.random.randint(next(ks), (BATCH, 1), 0, POS_OFFSET_MAX, jnp.int32)
                 + jnp.arange(SEQ, dtype=jnp.int32)[None, :])
    norm_g = 1.0 + nrm((L, 6, D_MODEL), 0.02)
    ffn_w_gate = nrm((L, 2, D_MODEL, D_FF), D_MODEL ** -0.5)
    ffn_w_up = nrm((L, 2, D_MODEL, D_FF), D_MODEL ** -0.5)
    ffn_w_down = nrm((L, 2, D_FF, D_MODEL), D_FF ** -0.5)
    w_in_first = nrm((D_MODEL, IN_COLS_FIRST), D_MODEL ** -0.5)
    w_in_rest = nrm((Lr, D_MODEL, IN_COLS_REST), D_MODEL ** -0.5)
    w_out = nrm((L, D_MIX, D_MODEL), D_MIX ** -0.5)
    rwkv_mu = unif((L, RWKV_COLS), 0.0, 1.0)
    rwkv_w0 = unif((L, D_RWKV), -6.0, 1.0)
    rwkv_w_up = nrm((L, RWKV_DECAY_RANK, D_RWKV), 0.5 * RWKV_DECAY_RANK ** -0.5)
    rwkv_a0 = nrm((L, D_RWKV), 0.1)
    rwkv_a_up = nrm((L, RWKV_ICLR_RANK, D_RWKV), 0.5 * RWKV_ICLR_RANK ** -0.5)
    rwkv_g_up = nrm((L, RWKV_GATE_RANK, D_RWKV), RWKV_GATE_RANK ** -0.5)
    rwkv_k_k = 1.0 + nrm((L, D_RWKV), 0.1)
    rwkv_k_a = 1.0 + nrm((L, D_RWKV), 0.1)
    rwkv_r_k = nrm((L, D_RWKV), 0.1)
    rwkv_ln_w = 1.0 + nrm((L, D_RWKV), 0.02)
    rwkv_ln_b = nrm((L, D_RWKV), 0.02)
    rwkv_mu_vres = unif((Lr, RWKV_VRES_RANK), 0.0, 1.0)
    rwkv_v0 = nrm((Lr, D_RWKV), 0.5)
    rwkv_v_up = nrm((Lr, RWKV_VRES_RANK, D_RWKV), 0.5 * RWKV_VRES_RANK ** -0.5)
    ret_gn_w = 1.0 + nrm((L, D_RET), 0.02)
    gdn_conv_w = nrm((L, GDN_CONV, 3 * D_GDN), GDN_CONV ** -0.5)
    gdn_A_log = jnp.log(unif((L, H_GDN), 1.0, 16.0))
    dt = jnp.exp(unif((L, H_GDN), math.log(1e-3), math.log(1e-1)))
    gdn_dt_bias = dt + jnp.log(-jnp.expm1(-dt))
    gdn_norm_w = 1.0 + nrm((L, HEAD_DIM), 0.02)
    return {'x': x, 'positions': positions, 'norm_g': norm_g,
            'ffn_w_gate': ffn_w_gate, 'ffn_w_up': ffn_w_up, 'ffn_w_down': ffn_w_down,
            'w_in_first': w_in_first, 'w_in_rest': w_in_rest, 'w_out': w_out,
            'rwkv_mu': rwkv_mu, 'rwkv_w0': rwkv_w0, 'rwkv_w_up': rwkv_w_up, 'rwkv_a0': rwkv_a0,
            'rwkv_a_up': rwkv_a_up, 'rwkv_g_up': rwkv_g_up, 'rwkv_k_k': rwkv_k_k, 'rwkv_k_a': rwkv_k_a,
            'rwkv_r_k': rwkv_r_k, 'rwkv_ln_w': rwkv_ln_w, 'rwkv_ln_b': rwkv_ln_b,
            'rwkv_mu_vres': rwkv_mu_vres, 'rwkv_v0': rwkv_v0, 'rwkv_v_up': rwkv_v_up,
            'ret_gn_w': ret_gn_w, 'gdn_conv_w': gdn_conv_w, 'gdn_A_log': gdn_A_log,
            'gdn_dt_bias': gdn_dt_bias, 'gdn_norm_w': gdn_norm_w}


def reference(x, positions, norm_g, ffn_w_gate, ffn_w_up, ffn_w_down, w_in_first, w_in_rest, w_out,
              rwkv_mu, rwkv_w0, rwkv_w_up, rwkv_a0, rwkv_a_up, rwkv_g_up, rwkv_k_k, rwkv_k_a, rwkv_r_k,
              rwkv_ln_w, rwkv_ln_b, rwkv_mu_vres, rwkv_v0, rwkv_v_up, ret_gn_w, gdn_conv_w, gdn_A_log,
              gdn_dt_bias, gdn_norm_w):
    inv_freq = ROPE_THETA ** (-jnp.arange(0, HEAD_DIM, 2, dtype=jnp.float32) / HEAD_DIM)
    ang = positions.astype(jnp.float32)[..., None] * inv_freq
    cos, sin = jnp.cos(ang)[:, :, None, :], jnp.sin(ang)[:, :, None, :]
    v_first = None
    for l in range(DEPTH):
        ng = norm_g[l]
        x = x + 0.5 * rms_norm(swiglu(rms_norm(x, ng[0]), ffn_w_gate[l, 0], ffn_w_up[l, 0], ffn_w_down[l, 0]), ng[1])
        h = rms_norm(x, ng[2])
        w_in = w_in_first if l == 0 else w_in_rest[l - 1]
        p = (h @ w_in).astype(jnp.float32)
        p_rwkv = p[..., :RWKV_COLS]
        p_ret = p[..., RWKV_COLS:RWKV_COLS + RET_COLS]
        p_gdn = p[..., RWKV_COLS + RET_COLS:IN_COLS_FIRST]
        vres = None if l == 0 else (p[..., IN_COLS_FIRST:], rwkv_mu_vres[l - 1], rwkv_v0[l - 1], rwkv_v_up[l - 1])
        y_a, v_first = rwkv7_time_mix(p_rwkv, vres, v_first, rwkv_mu[l], rwkv_w0[l], rwkv_w_up[l], rwkv_a0[l],
                                      rwkv_a_up[l], rwkv_g_up[l], rwkv_k_k[l], rwkv_k_a[l], rwkv_r_k[l],
                                      rwkv_ln_w[l], rwkv_ln_b[l])
        y_b = retention(p_ret, cos, sin, ret_gn_w[l])
        y_c = gated_deltanet(p_gdn, gdn_conv_w[l], gdn_A_log[l], gdn_dt_bias[l], gdn_norm_w[l])
        y = jnp.concatenate([y_a, y_b, y_c], axis=-1).astype(x.dtype) @ w_out[l]
        x = x + rms_norm(y, ng[3])
        x = x + 0.5 * rms_norm(swiglu(rms_norm(x, ng[4]), ffn_w_gate[l, 1], ffn_w_up[l, 1], ffn_w_down[l, 1]), ng[5])
    return x
```

```python
import functools
import math

import jax
import jax.numpy as jnp
from jax import lax
from jax.experimental import pallas as pl
from jax.experimental.pallas import tpu as pltpu

F32 = jnp.float32
BF16 = jnp.bfloat16

D_MODEL = 1024
D_FF = 2816
HEAD_DIM = 64
H_RWKV, H_RET, H_GDN = 6, 4, 6
D_RWKV, D_RET, D_GDN = H_RWKV * HEAD_DIM, H_RET * HEAD_DIM, H_GDN * HEAD_DIM
RWKV_DECAY_RANK, RWKV_ICLR_RANK, RWKV_VRES_RANK, RWKV_GATE_RANK = 64, 64, 32, 128
RWKV_COLS = 3 * D_RWKV + RWKV_DECAY_RANK + RWKV_ICLR_RANK + RWKV_GATE_RANK
RET_COLS = 4 * D_RET
GDN_MAIN_COLS = 4 * D_GDN
GDN_COLS = GDN_MAIN_COLS + 2 * H_GDN
IN_COLS_FIRST = RWKV_COLS + RET_COLS + GDN_COLS
GDN_CONV = 4
NORM_EPS = 1e-6
L2_EPS = 1e-6
RWKV_GN_EPS = HEAD_DIM * 1e-5
RET_GN_EPS = 1e-5
ROPE_THETA = 10000.0

LANES = 128
SUBLANES = 8
MISC_COLS = LANES
MISC_BETA, MISC_ALOG, MISC_VRES = 0, H_GDN, 2 * H_GDN
CHUNK = 64
RET_CHUNK = 128
PAIR = 2 * HEAD_DIM
ROW_TILE = 256
FF_CHUNK = 512
VMEM_LIMIT = 56 * 1024 * 1024


def _dot(a, b):
    return jnp.dot(a.astype(BF16), b.astype(BF16), preferred_element_type=F32)


def _dot_nt(a, b):
    return lax.dot_general(a.astype(BF16), b.astype(BF16), (((1,), (1,)), ((), ())),
                           preferred_element_type=F32)


def _dot_tn(a, b):
    return lax.dot_general(a.astype(BF16), b.astype(BF16), (((0,), (0,)), ((), ())),
                           preferred_element_type=F32)


def _split(x, parts):
    out = []
    for _ in range(parts - 1):
        h = x.astype(BF16)
        out.append(h)
        x = x - h.astype(F32)
    out.append(x.astype(BF16))
    return out


def _dot_exact_rhs(a_bf16, x, parts):
    return sum(jnp.dot(a_bf16, p, preferred_element_type=F32) for p in _split(x, parts))


def _dot_exact_lhs(x, b_bf16, parts):
    return sum(jnp.dot(p, b_bf16, preferred_element_type=F32) for p in _split(x, parts))


def _dot_hi(a, b):
    ah, al = _split(a, 2)
    bh, bl = _split(b, 2)
    return (jnp.dot(ah, bh, preferred_element_type=F32) + jnp.dot(ah, bl, preferred_element_type=F32)
            + jnp.dot(al, bh, preferred_element_type=F32))


def _rms(x, g):
    return x * lax.rsqrt(jnp.mean(x * x, axis=-1, keepdims=True) + NORM_EPS) * g


def _sigmoid(x):
    return 1.0 / (1.0 + jnp.exp(-x))


def _silu(x):
    return x * _sigmoid(x)


def _softplus(x):
    return jnp.maximum(x, 0.0) + jnp.log(1.0 + jnp.exp(-jnp.abs(x)))


def _stack_heads(x):
    lane = lax.broadcasted_iota(jnp.int32, x.shape, 1)
    return jnp.concatenate([jnp.where(lane < HEAD_DIM, x, 0.0), jnp.where(lane >= HEAD_DIM, x, 0.0)], axis=0)


def _pair_masks(c):
    i = lax.broadcasted_iota(jnp.int32, (2 * c, 2 * c), 0)
    j = lax.broadcasted_iota(jnp.int32, (2 * c, 2 * c), 1)
    same = ((i < c) & (j < c)) | ((i >= c) & (j >= c))
    return same & (i > j), same & (i >= j), (i == j)


def _tri_inv(a, eye):
    t = eye + a
    p = a
    for _ in range(int(math.log2(CHUNK)) - 1):
        p = _dot_hi(p, p)
        t = t + _dot_hi(p, t)
    return t


def _ffn_body(x_ref, gpre_ref, gpost_ref, wg_ref, wu_ref, wd_ref, o_ref, h_ref, a_ref):
    x = x_ref[...]
    h_ref[...] = _rms(x, gpre_ref[...]).astype(BF16)
    for c0 in range(0, D_FF, FF_CHUNK):
        w = min(FF_CHUNK, D_FF - c0)
        h = h_ref[...]
        g = jnp.dot(h, wg_ref[:, c0:c0 + w], preferred_element_type=F32)
        u = jnp.dot(h, wu_ref[:, c0:c0 + w], preferred_element_type=F32)
        a_ref[:, c0:c0 + w] = (_silu(g) * u).astype(BF16)
    y = jnp.dot(a_ref[...], wd_ref[...], preferred_element_type=F32)
    o_ref[...] = x + 0.5 * _rms(y, gpost_ref[...])


def _row_spec(width):
    return pl.BlockSpec((ROW_TILE, width), lambda i: (i, 0))


def _const_spec(shape):
    zeros = (0,) * len(shape)
    return pl.BlockSpec(shape, lambda *_: zeros, pipeline_mode=pl.Buffered(1))


def _dense_params():
    return pltpu.CompilerParams(dimension_semantics=("parallel",), vmem_limit_bytes=VMEM_LIMIT)


def _ffn(x, g_pre, g_post, wg, wu, wd):
    n = x.shape[0]
    return pl.pallas_call(
        _ffn_body,
        grid=(n // ROW_TILE,),
        in_specs=[_row_spec(D_MODEL), _const_spec((1, D_MODEL)), _const_spec((1, D_MODEL)),
                  _const_spec((D_MODEL, D_FF)), _const_spec((D_MODEL, D_FF)), _const_spec((D_FF, D_MODEL))],
        out_specs=_row_spec(D_MODEL),
        out_shape=jax.ShapeDtypeStruct((n, D_MODEL), F32),
        scratch_shapes=[pltpu.VMEM((ROW_TILE, D_MODEL), BF16), pltpu.VMEM((ROW_TILE, D_FF), BF16)],
        compiler_params=_dense_params(),
        name="ffn",
    )(x, g_pre, g_post, wg, wu, wd)


_IN_WIDTHS = (RWKV_COLS, RET_COLS, GDN_MAIN_COLS, MISC_COLS)
IN_COLS_PACKED = sum(_IN_WIDTHS)


def _inproj_body(x_ref, g_ref, w_ref, o_rwkv, o_ret, o_gdn, o_misc, h_ref):
    h_ref[...] = _rms(x_ref[...], g_ref[...]).astype(BF16)
    off = 0
    for o_ref in (o_rwkv, o_ret, o_gdn, o_misc):
        width = o_ref.shape[-1]
        for c0 in range(0, width, FF_CHUNK):
            w = min(FF_CHUNK, width - c0)
            o_ref[:, c0:c0 + w] = jnp.dot(h_ref[...], w_ref[:, off + c0:off + c0 + w],
                                          preferred_element_type=F32)
        off += width


def _inproj(x, g, w):
    n = x.shape[0]
    return pl.pallas_call(
        _inproj_body,
        grid=(n // ROW_TILE,),
        in_specs=[_row_spec(D_MODEL), _const_spec((1, D_MODEL)), _const_spec((D_MODEL, IN_COLS_PACKED))],
        out_specs=[_row_spec(w_) for w_ in _IN_WIDTHS],
        out_shape=[jax.ShapeDtypeStruct((n, w_), F32) for w_ in _IN_WIDTHS],
        scratch_shapes=[pltpu.VMEM((ROW_TILE, D_MODEL), BF16)],
        compiler_params=_dense_params(),
        name="inproj",
    )(x, g, w)


def _outproj_body(x_ref, ya_ref, yb_ref, yc_ref, wa_ref, wb_ref, wc_ref, g_ref, o_ref):
    y = (jnp.dot(ya_ref[...].astype(BF16), wa_ref[...], preferred_element_type=F32)
         + jnp.dot(yb_ref[...].astype(BF16), wb_ref[...], preferred_element_type=F32)
         + jnp.dot(yc_ref[...].astype(BF16), wc_ref[...], preferred_element_type=F32))
    o_ref[...] = x_ref[...] + _rms(y, g_ref[...])


def _outproj(x, ya, yb, yc, wa, wb, wc, g):
    n = x.shape[0]
    return pl.pallas_call(
        _outproj_body,
        grid=(n // ROW_TILE,),
        in_specs=[_row_spec(D_MODEL), _row_spec(D_RWKV), _row_spec(D_RET), _row_spec(D_GDN),
                  _const_spec((D_RWKV, D_MODEL)), _const_spec((D_RET, D_MODEL)), _const_spec((D_GDN, D_MODEL)),
                  _const_spec((1, D_MODEL))],
        out_specs=_row_spec(D_MODEL),
        out_shape=jax.ShapeDtypeStruct((n, D_MODEL), F32),
        compiler_params=_dense_params(),
        name="outproj",
    )(x, ya, yb, yc, wa, wb, wc, g)


def _rope_body(pos_ref, invf_ref, sign_ref, cos_ref, sin_ref):
    ang = pos_ref[0] * invf_ref[...]
    cos_ref[0] = jnp.cos(ang)
    sin_ref[0] = jnp.sin(ang) * sign_ref[...]


def _rope_tables(positions):
    b, t = positions.shape
    half = HEAD_DIM // 2
    inv_freq = ROPE_THETA ** (-jnp.arange(0, HEAD_DIM, 2, dtype=F32) / HEAD_DIM)
    invf = jnp.tile(inv_freq, LANES // half)[None, :]
    sign = jnp.tile(jnp.concatenate([-jnp.ones((half,), F32), jnp.ones((half,), F32)]), LANES // HEAD_DIM)[None, :]
    pos = positions.astype(F32)[..., None]
    tt = min(t, 512)
    return pl.pallas_call(
        _rope_body,
        grid=(b, t // tt),
        in_specs=[pl.BlockSpec((1, tt, 1), lambda i, j: (i, j, 0)),
                  pl.BlockSpec((1, LANES), lambda i, j: (0, 0)),
                  pl.BlockSpec((1, LANES), lambda i, j: (0, 0))],
        out_specs=[pl.BlockSpec((1, tt, LANES), lambda i, j: (i, j, 0))] * 2,
        out_shape=[jax.ShapeDtypeStruct((b, t, LANES), F32)] * 2,
        compiler_params=pltpu.CompilerParams(dimension_semantics=("parallel", "parallel")),
        name="rope_tables",
    )(pos, invf, sign)


def _seq_spec(c, width):
    return pl.BlockSpec((1, c, width), lambda b, t: (b, t, 0))


def _seq_const(shape):
    zeros = (0,) * len(shape)
    return pl.BlockSpec(shape, lambda b, t: zeros)


def _seq_params():
    return pltpu.CompilerParams(dimension_semantics=("parallel", "arbitrary"), vmem_limit_bytes=VMEM_LIMIT)


def _shifted(buf_ref, x, t, shifts):
    c = x.shape[0]

    @pl.when(t == 0)
    def _():
        buf_ref[0:SUBLANES, :] = jnp.zeros((SUBLANES, x.shape[1]), F32)

    buf_ref[SUBLANES:SUBLANES + c, :] = x
    out = [buf_ref[SUBLANES - s:SUBLANES - s + c, :] for s in shifts]
    buf_ref[0:SUBLANES, :] = x[c - SUBLANES:c, :]
    return out


def _rwkv_body(has_vres, *refs):
    if has_vres:
        (p_ref, misc_ref, vf_ref, mu_ref, w0_ref, wup_ref, a0_ref, aup_ref, gup_ref, kk_ref, ka_ref, rk_ref,
         lnw_ref, lnb_ref, muv_ref, v0_ref, vup_ref, tril_ref, blk_ref,
         y_ref, buf_ref, bufv_ref, s_ref) = refs
    else:
        (p_ref, mu_ref, w0_ref, wup_ref, a0_ref, aup_ref, gup_ref, kk_ref, ka_ref, rk_ref,
         lnw_ref, lnb_ref, tril_ref, blk_ref,
         y_ref, vf_out_ref, buf_ref, s_ref) = refs
    t = pl.program_id(1)
    c = CHUNK

    @pl.when(t == 0)
    def _():
        s_ref[...] = jnp.zeros(s_ref.shape, F32)

    p = p_ref[0]
    (prev,) = _shifted(buf_ref, p, t, (1,))
    ps = p + (prev - p) * mu_ref[...]
    r = ps[:, 0:D_RWKV]
    k = ps[:, D_RWKV:2 * D_RWKV]
    v = ps[:, 2 * D_RWKV:3 * D_RWKV]
    wa = ps[:, 3 * D_RWKV:3 * D_RWKV + LANES]
    gc = ps[:, 3 * D_RWKV + LANES:RWKV_COLS]
    logw = -math.exp(-0.5) * _sigmoid(w0_ref[...] + _dot(jnp.tanh(wa), wup_ref[...]))
    a = _sigmoid(a0_ref[...] + _dot(wa, aup_ref[...]))
    g = _dot(_sigmoid(gc), gup_ref[...])
    if has_vres:
        pm = misc_ref[0]
        (prevm,) = _shifted(bufv_ref, pm, t, (1,))
        pmv = pm + (prevm - pm) * muv_ref[...]
        v = v + (vf_ref[0] - v) * _sigmoid(v0_ref[...] + _dot(pmv, vup_ref[...]))
    else:
        vf_out_ref[0] = v

    blk = blk_ref[...]
    kkr = k * kk_ref[...]
    kk = kkr * lax.rsqrt(_dot_exact_lhs(kkr * kkr, blk, 2) + L2_EPS)
    k2 = k * (1.0 + (a - 1.0) * ka_ref[...])

    cw = _dot_exact_rhs(tril_ref[...], logw, 3)
    cw_last = cw[c - 1:c, :]
    dec_in = jnp.exp(cw)
    dec_ex = jnp.exp(cw - logw)
    dec_inv = jnp.exp(-cw)
    dec_tail = jnp.exp(cw_last - cw)
    dec_all = jnp.exp(cw_last)
    kka = kk * a
    b_p = kk * dec_ex
    r_p = r * dec_in
    a_i = -kka * dec_inv
    k_i = k2 * dec_inv
    a_d = -kka * dec_tail
    k_d = k2 * dec_tail

    strict, tril, diag = _pair_masks(c)
    eye = jnp.where(diag, 1.0, 0.0)
    ys = []
    for h in range(H_RWKV // 2):
        sl = slice(h * PAIR, (h + 1) * PAIR)
        lb = _stack_heads(b_p[:, sl])
        lr = _stack_heads(r_p[:, sl])
        ra = jnp.concatenate([a_i[:, sl], a_i[:, sl]], axis=0)
        rk = jnp.concatenate([k_i[:, sl], k_i[:, sl]], axis=0)
        v_st = _stack_heads(v[:, sl])
        m_a = jnp.where(strict, _dot_nt(lb, ra), 0.0)
        m_b = jnp.where(strict, _dot_nt(lb, rk), 0.0)
        m_aq = jnp.where(tril, _dot_nt(lr, ra), 0.0)
        m_bq = jnp.where(tril, _dot_nt(lr, rk), 0.0)
        s = s_ref[h]
        u = _dot(_tri_inv(m_a, eye), _dot_nt(lb, s) + _dot(m_b, v_st))
        y_st = _dot_nt(lr, s) + _dot(m_aq, u) + _dot(m_bq, v_st)
        s_ref[h] = (s * dec_all[:, sl] + _dot_tn(u, _stack_heads(a_d[:, sl]))
                    + _dot_tn(v_st, _stack_heads(k_d[:, sl])))
        ys.append(y_st[0:c] + y_st[c:2 * c])
    y = jnp.concatenate(ys, axis=1)

    inv_d = 1.0 / HEAD_DIM
    mean = _dot_exact_lhs(y, blk, 2) * inv_d
    yc = y - mean
    var = _dot_exact_lhs(yc * yc, blk, 2) * inv_d
    yn = yc * lax.rsqrt(var + RWKV_GN_EPS) * lnw_ref[...] + lnb_ref[...]
    bonus = _dot_exact_lhs(r * k2 * rk_ref[...], blk, 2) * v
    y_ref[0] = (yn + bonus) * g


def _rwkv(p_rwkv, misc, v_first, prm, consts):
    b, t, _ = p_rwkv.shape
    c = CHUNK
    has_vres = v_first is not None
    row = lambda n: _seq_const((1, n))
    common = [prm["mu"], prm["w0"], prm["w_up"], prm["a0"], prm["a_up"], prm["g_up"], prm["k_k"], prm["k_a"],
              prm["r_k"], prm["ln_w"], prm["ln_b"]]
    common_specs = [row(RWKV_COLS), row(D_RWKV), _seq_const((LANES, D_RWKV)), row(D_RWKV),
                    _seq_const((LANES, D_RWKV)), _seq_const((RWKV_GATE_RANK, D_RWKV)), row(D_RWKV), row(D_RWKV),
                    row(D_RWKV), row(D_RWKV), row(D_RWKV)]
    const_in = [consts["tril"], consts["blk_rwkv"]]
    const_specs = [_seq_const((c, c)), _seq_const((D_RWKV, D_RWKV))]
    state = pltpu.VMEM((H_RWKV // 2, PAIR, PAIR), F32)
    if has_vres:
        ins = [p_rwkv, misc, v_first] + common + [prm["mu_v"], prm["v0"], prm["v_up"]] + const_in
        in_specs = ([_seq_spec(c, RWKV_COLS), _seq_spec(c, MISC_COLS), _seq_spec(c, D_RWKV)] + common_specs
                    + [row(MISC_COLS), row(D_RWKV), _seq_const((MISC_COLS, D_RWKV))] + const_specs)
        out_specs = _seq_spec(c, D_RWKV)
        out_shape = jax.ShapeDtypeStruct((b, t, D_RWKV), F32)
        scratch = [pltpu.VMEM((SUBLANES + c, RWKV_COLS), F32), pltpu.VMEM((SUBLANES + c, MISC_COLS), F32), state]
    else:
        ins = [p_rwkv] + common + const_in
        in_specs = [_seq_spec(c, RWKV_COLS)] + common_specs + const_specs
        out_specs = [_seq_spec(c, D_RWKV)] * 2
        out_shape = [jax.ShapeDtypeStruct((b, t, D_RWKV), F32)] * 2
        scratch = [pltpu.VMEM((SUBLANES + c, RWKV_COLS), F32), state]
    out = pl.pallas_call(
        functools.partial(_rwkv_body, has_vres),
        grid=(b, t // c),
        in_specs=in_specs, out_specs=out_specs, out_shape=out_shape, scratch_shapes=scratch,
        compiler_params=_seq_params(),
        name="rwkv7_vres" if has_vres else "rwkv7_first",
    )(*ins)
    return (out, v_first) if has_vres else (out[0], out[1])


def _ret_body(p_ref, cos_ref, sin_ref, gnw_ref, dec_ref, qw_ref, kw_ref, cd_ref, blk_ref, y_ref, r_ref):
    t = pl.program_id(1)

    @pl.when(t == 0)
    def _():
        r_ref[...] = jnp.zeros(r_ref.shape, F32)

    p = p_ref[0]
    q = p[:, 0:D_RET]
    k = p[:, D_RET:2 * D_RET]
    v = p[:, 2 * D_RET:3 * D_RET]
    gate = p[:, 3 * D_RET:4 * D_RET]
    cos = jnp.concatenate([cos_ref[0]] * (D_RET // LANES), axis=1)
    sin = jnp.concatenate([sin_ref[0]] * (D_RET // LANES), axis=1)
    lane = lax.broadcasted_iota(jnp.int32, q.shape, 1)
    first_half = (lane % HEAD_DIM) < (HEAD_DIM // 2)
    half = HEAD_DIM // 2

    def rope(x):
        partner = jnp.where(first_half, pltpu.roll(x, D_RET - half, 1), pltpu.roll(x, half, 1))
        return x * cos + partner * sin

    q = rope(q) * HEAD_DIM ** -0.5
    k = rope(k)
    blk = blk_ref[...]
    o = _dot(q * qw_ref[...], r_ref[...])
    for h in range(H_RET):
        in_head = (lane // HEAD_DIM) == h
        scores = _dot_nt(jnp.where(in_head, q, 0.0), k) * dec_ref[h]
        o = o + _dot(scores, jnp.where(in_head, v, 0.0))
    r_ref[...] = r_ref[...] * cd_ref[...] + _dot_tn(k * kw_ref[...], v) * blk.astype(F32)

    inv_d = 1.0 / HEAD_DIM
    mean = _dot_exact_lhs(o, blk, 2) * inv_d
    oc = o - mean
    var = _dot_exact_lhs(oc * oc, blk, 2) * inv_d
    y_ref[0] = _silu(gate) * (oc * lax.rsqrt(var + RET_GN_EPS) * gnw_ref[...])


def _ret(p_ret, cos, sin, gn_w, consts):
    b, t, _ = p_ret.shape
    c = RET_CHUNK
    return pl.pallas_call(
        _ret_body,
        grid=(b, t // c),
        in_specs=[_seq_spec(c, RET_COLS), _seq_spec(c, LANES), _seq_spec(c, LANES), _seq_const((1, D_RET)),
                  _seq_const((H_RET, c, c)), _seq_const((c, D_RET)), _seq_const((c, D_RET)), _seq_const((1, D_RET)),
                  _seq_const((D_RET, D_RET))],
        out_specs=_seq_spec(c, D_RET),
        out_shape=jax.ShapeDtypeStruct((b, t, D_RET), F32),
        scratch_shapes=[pltpu.VMEM((D_RET, D_RET), F32)],
        compiler_params=_seq_params(),
        name="retention",
    )(p_ret, cos, sin, gn_w, consts["ret_decay"], consts["ret_qw"], consts["ret_kw"], consts["ret_cd"],
      consts["blk_ret"])


def _gdn_body(p_ref, misc_ref, convw_ref, alog_ref, dtb_ref, nw_ref, tril_ref, blk_ref, eg_ref, eb_ref, el_ref,
              y_ref, buf_ref, s_ref):
    t = pl.program_id(1)
    c = CHUNK

    @pl.when(t == 0)
    def _():
        s_ref[...] = jnp.zeros(s_ref.shape, F32)

    p = p_ref[0]
    x = p[:, 0:3 * D_GDN]
    gate = p[:, 3 * D_GDN:4 * D_GDN]
    taps = _shifted(buf_ref, x, t, tuple(GDN_CONV - 1 - j for j in range(GDN_CONV)))
    conv = taps[0] * convw_ref[0:1, :]
    for j in range(1, GDN_CONV):
        conv = conv + taps[j] * convw_ref[j:j + 1, :]
    qkv = _silu(conv)
    blk = blk_ref[...]
    q = qkv[:, 0:D_GDN]
    k = qkv[:, D_GDN:2 * D_GDN]
    v = qkv[:, 2 * D_GDN:3 * D_GDN]
    q = q * lax.rsqrt(_dot_exact_lhs(q * q, blk, 2) + L2_EPS) * HEAD_DIM ** -0.5
    k = k * lax.rsqrt(_dot_exact_lhs(k * k, blk, 2) + L2_EPS)

    m = misc_ref[0]
    beta = _sigmoid(m)
    g = -jnp.exp(alog_ref[...]) * _softplus(m + dtb_ref[...])
    gcs = _dot_exact_rhs(tril_ref[...], g, 3)
    g_wide = _dot_exact_lhs(gcs, eg_ref[...], 3)
    b_wide = _dot_exact_lhs(beta, eb_ref[...], 2)
    g_last = _dot_exact_lhs(gcs[c - SUBLANES:c, :], el_ref[...], 3)[SUBLANES - 1:SUBLANES, :]

    strict, tril, diag = _pair_masks(c)
    eye = jnp.where(diag, 1.0, 0.0)
    os_ = []
    for h in range(H_GDN // 2):
        sl = slice(h * PAIR, (h + 1) * PAIR)
        w0, w1 = 2 * h * LANES, (2 * h + 1) * LANES
        gc = jnp.concatenate([g_wide[:, w0:w0 + LANES], g_wide[:, w1:w1 + LANES]], axis=0)
        be = jnp.concatenate([b_wide[:, w0:w0 + LANES], b_wide[:, w1:w1 + LANES]], axis=0)
        gl = jnp.concatenate([jnp.broadcast_to(gc[c - 1:c, :], (c, LANES)),
                              jnp.broadcast_to(gc[2 * c - 1:2 * c, :], (c, LANES))], axis=0)
        q_st = _stack_heads(q[:, sl])
        k_st = _stack_heads(k[:, sl])
        v_st = _stack_heads(v[:, sl])
        decay = jnp.where(tril, jnp.exp(jnp.where(tril, gc - gc.T, 0.0)), 0.0)
        kb = k_st * be
        vb = v_st * be
        lmat = jnp.where(strict, _dot_nt(kb, k_st) * decay, 0.0)
        tm = _tri_inv(-lmat, eye)
        eg = jnp.exp(gc)
        u = _dot(tm, vb)
        w = _dot(tm, kb * eg)
        attn = _dot_nt(q_st, k_st) * decay
        k_tail = k_st * jnp.exp(gl - gc)
        s = s_ref[h]
        v_new = u - _dot(w, s)
        o_st = _dot(q_st * eg, s) + _dot(attn, v_new)
        s_ref[h] = s * jnp.exp(g_last[:, sl]) + _dot_tn(k_tail, v_new)
        os_.append(o_st[0:c] + o_st[c:2 * c])
    o = jnp.concatenate(os_, axis=1)
    ms = _dot_exact_lhs(o * o, blk, 2) * (1.0 / HEAD_DIM)
    y_ref[0] = o * lax.rsqrt(ms + NORM_EPS) * nw_ref[...] * _silu(gate)


def _gdn(p_gdn, misc, prm, consts):
    b, t, _ = p_gdn.shape
    c = CHUNK
    return pl.pallas_call(
        _gdn_body,
        grid=(b, t // c),
        in_specs=[_seq_spec(c, GDN_MAIN_COLS), _seq_spec(c, MISC_COLS), _seq_const((GDN_CONV, 3 * D_GDN)),
                  _seq_const((1, MISC_COLS)), _seq_const((1, MISC_COLS)), _seq_const((1, D_GDN)),
                  _seq_const((c, c)), _seq_const((D_GDN, D_GDN)), _seq_const((MISC_COLS, H_GDN * LANES)),
                  _seq_const((MISC_COLS, H_GDN * LANES)), _seq_const((MISC_COLS, D_GDN))],
        out_specs=_seq_spec(c, D_GDN),
        out_shape=jax.ShapeDtypeStruct((b, t, D_GDN), F32),
        scratch_shapes=[pltpu.VMEM((SUBLANES + c, 3 * D_GDN), F32), pltpu.VMEM((H_GDN // 2, PAIR, PAIR), F32)],
        compiler_params=_seq_params(),
        name="gated_deltanet",
    )(p_gdn, misc, prm["conv_w"], prm["a_log"], prm["dt_bias"], prm["norm_w"], consts["tril"], consts["blk_gdn"],
      consts["exp_g"], consts["exp_b"], consts["exp_l"])


def _block_ones(n):
    i = jnp.arange(n) // HEAD_DIM
    return (i[:, None] == i[None, :]).astype(BF16)


def _constants():
    c = CHUNK
    idx = jnp.arange(c)
    consts = {"tril": (idx[:, None] >= idx[None, :]).astype(BF16),
              "blk_rwkv": _block_ones(D_RWKV), "blk_ret": _block_ones(D_RET), "blk_gdn": _block_ones(D_GDN)}
    rc = RET_CHUNK
    ridx = jnp.arange(rc)
    log_gamma = jnp.log(1.0 - 2.0 ** (-5.0 - jnp.arange(H_RET, dtype=F32)))
    rel = ridx[:, None] - ridx[None, :]
    consts["ret_decay"] = jnp.where(rel >= 0, jnp.exp(log_gamma[:, None, None] * jnp.maximum(rel, 0)), 0.0)
    lanes_gamma = jnp.repeat(log_gamma, HEAD_DIM)[None, :]
    consts["ret_qw"] = jnp.exp(lanes_gamma * (ridx[:, None] + 1))
    consts["ret_kw"] = jnp.exp(lanes_gamma * (rc - 1 - ridx[:, None]))
    consts["ret_cd"] = jnp.exp(lanes_gamma * rc)
    rows = jnp.arange(MISC_COLS)[:, None]
    wide_head = jnp.arange(H_GDN * LANES)[None, :] // LANES
    lane_head = jnp.arange(D_GDN)[None, :] // HEAD_DIM
    consts["exp_g"] = (rows == MISC_ALOG + wide_head).astype(BF16)
    consts["exp_b"] = (rows == MISC_BETA + wide_head).astype(BF16)
    consts["exp_l"] = (rows == MISC_ALOG + lane_head).astype(BF16)
    return consts


def _pad_rows(w, top, total):
    return jnp.pad(w, ((top, total - top - w.shape[0]), (0, 0)))


def _pad_lanes(v, left, total):
    return jnp.pad(v, (left, total - left - v.shape[0]))[None, :]


def kernel(x, positions, norm_g, ffn_w_gate, ffn_w_up, ffn_w_down, w_in_first, w_in_rest, w_out, rwkv_mu, rwkv_w0, rwkv_w_up, rwkv_a0, rwkv_a_up, rwkv_g_up, rwkv_k_k, rwkv_k_a, rwkv_r_k, rwkv_ln_w, rwkv_ln_b, rwkv_mu_vres, rwkv_v0, rwkv_v_up, ret_gn_w, gdn_conv_w, gdn_A_log, gdn_dt_bias, gdn_norm_w):
    b, t, d = x.shape
    depth = norm_g.shape[0]
    n = b * t
    assert d == D_MODEL and n % ROW_TILE == 0 and t % RET_CHUNK == 0 and t % CHUNK == 0
    consts = _constants()
    cos, sin = _rope_tables(positions)
    x = x.reshape(n, d)
    v_first = None
    for l in range(depth):
        ng = norm_g[l][:, None, :]
        wg, wu, wd = (w[l].astype(BF16) for w in (ffn_w_gate, ffn_w_up, ffn_w_down))
        x = _ffn(x, ng[0], ng[1], wg[0], wu[0], wd[0])

        w_in = w_in_first if l == 0 else w_in_rest[l - 1]
        gdn_small = w_in[:, RWKV_COLS + RET_COLS + GDN_MAIN_COLS:IN_COLS_FIRST]
        vres_cols = w_in[:, IN_COLS_FIRST:] if l > 0 else jnp.zeros((d, RWKV_VRES_RANK), F32)
        misc_w = jnp.concatenate([gdn_small, vres_cols], axis=1)
        misc_w = jnp.pad(misc_w, ((0, 0), (0, MISC_COLS - misc_w.shape[1])))
        w_packed = jnp.concatenate([w_in[:, :RWKV_COLS + RET_COLS + GDN_MAIN_COLS], misc_w], axis=1).astype(BF16)
        p_rwkv, p_ret, p_gdn, p_misc = _inproj(x, ng[2], w_packed)
        p_rwkv = p_rwkv.reshape(b, t, RWKV_COLS)
        p_ret = p_ret.reshape(b, t, RET_COLS)
        p_gdn = p_gdn.reshape(b, t, GDN_MAIN_COLS)
        p_misc = p_misc.reshape(b, t, MISC_COLS)

        rw = {"mu": rwkv_mu[l][None, :], "w0": rwkv_w0[l][None, :],
              "w_up": _pad_rows(rwkv_w_up[l], 0, LANES).astype(BF16), "a0": rwkv_a0[l][None, :],
              "a_up": _pad_rows(rwkv_a_up[l], RWKV_DECAY_RANK, LANES).astype(BF16),
              "g_up": rwkv_g_up[l].astype(BF16), "k_k": rwkv_k_k[l][None, :], "k_a": rwkv_k_a[l][None, :],
              "r_k": rwkv_r_k[l][None, :], "ln_w": rwkv_ln_w[l][None, :], "ln_b": rwkv_ln_b[l][None, :]}
        if l > 0:
            rw["mu_v"] = _pad_lanes(rwkv_mu_vres[l - 1], MISC_VRES, MISC_COLS)
            rw["v0"] = rwkv_v0[l - 1][None, :]
            rw["v_up"] = _pad_rows(rwkv_v_up[l - 1], MISC_VRES, MISC_COLS).astype(BF16)
        y_a, v_first = _rwkv(p_rwkv, p_misc, v_first, rw, consts)
        y_b = _ret(p_ret, cos, sin, ret_gn_w[l][None, :], consts)
        gd = {"conv_w": gdn_conv_w[l], "a_log": _pad_lanes(gdn_A_log[l], MISC_ALOG, MISC_COLS),
              "dt_bias": _pad_lanes(gdn_dt_bias[l], MISC_ALOG, MISC_COLS),
              "norm_w": jnp.tile(gdn_norm_w[l], H_GDN)[None, :]}
        y_c = _gdn(p_gdn, p_misc, gd, consts)

        wo = w_out[l].astype(BF16)
        x = _outproj(x, y_a.reshape(n, D_RWKV), y_b.reshape(n, D_RET), y_c.reshape(n, D_GDN),
                     wo[:D_RWKV], wo[D_RWKV:D_RWKV + D_RET], wo[D_RWKV + D_RET:], ng[3])
        x = _ffn(x, ng[4], ng[5], wg[1], wu[1], wd[1])
    return x.reshape(b, t, d)
```

```python
import functools
import math

import jax
import jax.numpy as jnp
from jax import lax
from jax.experimental import pallas as pl
from jax.experimental.pallas import tpu as pltpu

F32 = jnp.float32
BF16 = jnp.bfloat16

D_MODEL = 1024
D_FF = 2816
HEAD_DIM = 64
H_RWKV, H_RET, H_GDN = 6, 4, 6
D_RWKV, D_RET, D_GDN = H_RWKV * HEAD_DIM, H_RET * HEAD_DIM, H_GDN * HEAD_DIM
RWKV_DECAY_RANK, RWKV_ICLR_RANK, RWKV_VRES_RANK, RWKV_GATE_RANK = 64, 64, 32, 128
RWKV_COLS = 3 * D_RWKV + RWKV_DECAY_RANK + RWKV_ICLR_RANK + RWKV_GATE_RANK
RET_COLS = 4 * D_RET
GDN_MAIN_COLS = 4 * D_GDN
GDN_COLS = GDN_MAIN_COLS + 2 * H_GDN
IN_COLS_FIRST = RWKV_COLS + RET_COLS + GDN_COLS
GDN_CONV = 4
NORM_EPS = 1e-6
L2_EPS = 1e-6
RWKV_GN_EPS = HEAD_DIM * 1e-5
RET_GN_EPS = 1e-5
ROPE_THETA = 10000.0

LANES = 128
SUBLANES = 8
MISC_COLS = LANES
MISC_BETA, MISC_ALOG, MISC_VRES = 0, H_GDN, 2 * H_GDN
CHUNK = 64
RET_CHUNK = 128
PAIR = 2 * HEAD_DIM
SEQ_ROWS = 4
ROW_TILE = 256
FF_CHUNK = 512
VMEM_LIMIT = 56 * 1024 * 1024


def _dot(a, b):
    return jnp.dot(a.astype(BF16), b.astype(BF16), preferred_element_type=F32)


def _dot_nt(a, b):
    return lax.dot_general(a.astype(BF16), b.astype(BF16), (((1,), (1,)), ((), ())),
                           preferred_element_type=F32)


def _dot_tn(a, b):
    return lax.dot_general(a.astype(BF16), b.astype(BF16), (((0,), (0,)), ((), ())),
                           preferred_element_type=F32)


def _split(x, parts):
    out = []
    for _ in range(parts - 1):
        h = x.astype(BF16)
        out.append(h)
        x = x - h.astype(F32)
    out.append(x.astype(BF16))
    return out


def _dot_exact_rhs(a_bf16, x, parts):
    return sum(jnp.dot(a_bf16, p, preferred_element_type=F32) for p in _split(x, parts))


def _dot_exact_lhs(x, b_bf16, parts):
    return sum(jnp.dot(p, b_bf16, preferred_element_type=F32) for p in _split(x, parts))


def _rms(x, g):
    return x * lax.rsqrt(jnp.mean(x * x, axis=-1, keepdims=True) + NORM_EPS) * g


def _sigmoid(x):
    return 1.0 / (1.0 + jnp.exp(-x))


def _silu(x):
    return x * _sigmoid(x)


def _softplus(x):
    return jnp.maximum(x, 0.0) + jnp.log(1.0 + jnp.exp(-jnp.abs(x)))


def _stack_heads(x):
    lane = lax.broadcasted_iota(jnp.int32, x.shape, 1)
    return jnp.concatenate([jnp.where(lane < HEAD_DIM, x, 0.0), jnp.where(lane >= HEAD_DIM, x, 0.0)], axis=0)


def _pair_masks(c):
    i = lax.broadcasted_iota(jnp.int32, (2 * c, 2 * c), 0)
    j = lax.broadcasted_iota(jnp.int32, (2 * c, 2 * c), 1)
    same = ((i < c) & (j < c)) | ((i >= c) & (j >= c))
    return same & (i > j), same & (i >= j), (i == j)


NEUMANN_LEVELS = int(math.log2(CHUNK))


def _power_step(p, x, last):
    n = p.shape[1]
    if last:
        return None, x + jnp.dot(p, x.astype(BF16), preferred_element_type=F32)
    z = jnp.dot(p, jnp.concatenate([p, x.astype(BF16)], axis=1), preferred_element_type=F32)
    return z[:, :n].astype(BF16), x + z[:, n:]


def _ffn_body(x_ref, gpre_ref, gpost_ref, wg_ref, wu_ref, wd_ref, o_ref, h_ref, a_ref):
    x = x_ref[...]
    h_ref[...] = _rms(x, gpre_ref[...]).astype(BF16)
    for c0 in range(0, D_FF, FF_CHUNK):
        w = min(FF_CHUNK, D_FF - c0)
        h = h_ref[...]
        g = jnp.dot(h, wg_ref[:, c0:c0 + w], preferred_element_type=F32)
        u = jnp.dot(h, wu_ref[:, c0:c0 + w], preferred_element_type=F32)
        a_ref[:, c0:c0 + w] = (_silu(g) * u).astype(BF16)
    y = jnp.dot(a_ref[...], wd_ref[...], preferred_element_type=F32)
    o_ref[...] = x + 0.5 * _rms(y, gpost_ref[...])


def _row_spec(width):
    return pl.BlockSpec((ROW_TILE, width), lambda i: (i, 0))


def _const_spec(shape):
    zeros = (0,) * len(shape)
    return pl.BlockSpec(shape, lambda *_: zeros, pipeline_mode=pl.Buffered(1))


def _dense_params():
    return pltpu.CompilerParams(dimension_semantics=("parallel",), vmem_limit_bytes=VMEM_LIMIT)


def _ffn(x, g_pre, g_post, wg, wu, wd):
    n = x.shape[0]
    return pl.pallas_call(
        _ffn_body,
        grid=(n // ROW_TILE,),
        in_specs=[_row_spec(D_MODEL), _const_spec((1, D_MODEL)), _const_spec((1, D_MODEL)),
                  _const_spec((D_MODEL, D_FF)), _const_spec((D_MODEL, D_FF)), _const_spec((D_FF, D_MODEL))],
        out_specs=_row_spec(D_MODEL),
        out_shape=jax.ShapeDtypeStruct((n, D_MODEL), F32),
        scratch_shapes=[pltpu.VMEM((ROW_TILE, D_MODEL), BF16), pltpu.VMEM((ROW_TILE, D_FF), BF16)],
        compiler_params=_dense_params(),
        name="ffn",
    )(x, g_pre, g_post, wg, wu, wd)


_IN_WIDTHS = (RWKV_COLS, RET_COLS, GDN_MAIN_COLS, MISC_COLS)
IN_COLS_PACKED = sum(_IN_WIDTHS)


def _inproj_body(x_ref, g_ref, w_ref, o_rwkv, o_ret, o_gdn, o_misc, h_ref):
    h_ref[...] = _rms(x_ref[...], g_ref[...]).astype(BF16)
    off = 0
    for o_ref in (o_rwkv, o_ret, o_gdn, o_misc):
        width = o_ref.shape[-1]
        for c0 in range(0, width, FF_CHUNK):
            w = min(FF_CHUNK, width - c0)
            o_ref[:, c0:c0 + w] = jnp.dot(h_ref[...], w_ref[:, off + c0:off + c0 + w],
                                          preferred_element_type=F32)
        off += width


def _inproj(x, g, w):
    n = x.shape[0]
    return pl.pallas_call(
        _inproj_body,
        grid=(n // ROW_TILE,),
        in_specs=[_row_spec(D_MODEL), _const_spec((1, D_MODEL)), _const_spec((D_MODEL, IN_COLS_PACKED))],
        out_specs=[_row_spec(w_) for w_ in _IN_WIDTHS],
        out_shape=[jax.ShapeDtypeStruct((n, w_), F32) for w_ in _IN_WIDTHS],
        scratch_shapes=[pltpu.VMEM((ROW_TILE, D_MODEL), BF16)],
        compiler_params=_dense_params(),
        name="inproj",
    )(x, g, w)


def _outproj_body(x_ref, ya_ref, yb_ref, yc_ref, wa_ref, wb_ref, wc_ref, g_ref, o_ref):
    y = (jnp.dot(ya_ref[...].astype(BF16), wa_ref[...], preferred_element_type=F32)
         + jnp.dot(yb_ref[...].astype(BF16), wb_ref[...], preferred_element_type=F32)
         + jnp.dot(yc_ref[...].astype(BF16), wc_ref[...], preferred_element_type=F32))
    o_ref[...] = x_ref[...] + _rms(y, g_ref[...])


def _outproj(x, ya, yb, yc, wa, wb, wc, g):
    n = x.shape[0]
    return pl.pallas_call(
        _outproj_body,
        grid=(n // ROW_TILE,),
        in_specs=[_row_spec(D_MODEL), _row_spec(D_RWKV), _row_spec(D_RET), _row_spec(D_GDN),
                  _const_spec((D_RWKV, D_MODEL)), _const_spec((D_RET, D_MODEL)), _const_spec((D_GDN, D_MODEL)),
                  _const_spec((1, D_MODEL))],
        out_specs=_row_spec(D_MODEL),
        out_shape=jax.ShapeDtypeStruct((n, D_MODEL), F32),
        compiler_params=_dense_params(),
        name="outproj",
    )(x, ya, yb, yc, wa, wb, wc, g)


def _rope_body(pos_ref, invf_ref, sign_ref, cos_ref, sin_ref):
    ang = pos_ref[0] * invf_ref[...]
    cos_ref[0] = jnp.cos(ang)
    sin_ref[0] = jnp.sin(ang) * sign_ref[...]


def _rope_tables(positions):
    b, t = positions.shape
    half = HEAD_DIM // 2
    inv_freq = ROPE_THETA ** (-jnp.arange(0, HEAD_DIM, 2, dtype=F32) / HEAD_DIM)
    invf = jnp.tile(inv_freq, LANES // half)[None, :]
    sign = jnp.tile(jnp.concatenate([-jnp.ones((half,), F32), jnp.ones((half,), F32)]), LANES // HEAD_DIM)[None, :]
    pos = positions.astype(F32)[..., None]
    tt = min(t, 512)
    return pl.pallas_call(
        _rope_body,
        grid=(b, t // tt),
        in_specs=[pl.BlockSpec((1, tt, 1), lambda i, j: (i, j, 0)),
                  pl.BlockSpec((1, LANES), lambda i, j: (0, 0)),
                  pl.BlockSpec((1, LANES), lambda i, j: (0, 0))],
        out_specs=[pl.BlockSpec((1, tt, LANES), lambda i, j: (i, j, 0))] * 2,
        out_shape=[jax.ShapeDtypeStruct((b, t, LANES), F32)] * 2,
        compiler_params=pltpu.CompilerParams(dimension_semantics=("parallel", "parallel")),
        name="rope_tables",
    )(pos, invf, sign)


def _seq_spec(c, width, rows=1):
    return pl.BlockSpec((rows, c, width), lambda b, t: (b, t, 0))


def _seq_const(shape):
    zeros = (0,) * len(shape)
    return pl.BlockSpec(shape, lambda b, t: zeros)


def _seq_rows(b):
    return math.gcd(b, SEQ_ROWS)


def _seq_params():
    return pltpu.CompilerParams(dimension_semantics=("parallel", "arbitrary"), vmem_limit_bytes=VMEM_LIMIT)


def _shifted(buf_ref, x_ref, shifts):
    rows, c, _ = x_ref.shape
    width = buf_ref.shape[-1]
    out = [[] for _ in shifts]
    for r in range(rows):
        x = x_ref[r, :, 0:width]
        buf_ref[r, SUBLANES:SUBLANES + c, :] = x
        for i, s in enumerate(shifts):
            out[i].append(buf_ref[r, SUBLANES - s:SUBLANES - s + c, :])
        buf_ref[r, 0:SUBLANES, :] = x[c - SUBLANES:c, :]
    return [jnp.concatenate(o, axis=0) for o in out]


def _zero_tails(buf_ref):
    rows, _, width = buf_ref.shape
    buf_ref[:, 0:SUBLANES, :] = jnp.zeros((rows, SUBLANES, width), F32)


def _last_row_of_each_chunk(x, rows, c):
    return jnp.concatenate([jnp.broadcast_to(x[(r + 1) * c - 1:(r + 1) * c, :], (c, x.shape[1]))
                            for r in range(rows)], axis=0)


def _rwkv_body(has_vres, *refs):
    if has_vres:
        (p_ref, misc_ref, vf_ref, mu_ref, w0_ref, wup_ref, a0_ref, aup_ref, gup_ref, kk_ref, ka_ref, rk_ref,
         lnw_ref, lnb_ref, muv_ref, v0_ref, vup_ref, tril_ref, blk_ref,
         y_ref, buf_ref, bufv_ref, s_ref, ys_ref) = refs
    else:
        (p_ref, mu_ref, w0_ref, wup_ref, a0_ref, aup_ref, gup_ref, kk_ref, ka_ref, rk_ref,
         lnw_ref, lnb_ref, tril_ref, blk_ref,
         y_ref, vf_out_ref, buf_ref, s_ref, ys_ref) = refs
        bufv_ref = None
    rows, c, _ = p_ref.shape

    @pl.when(pl.program_id(1) == 0)
    def _():
        s_ref[...] = jnp.zeros(s_ref.shape, F32)
        _zero_tails(buf_ref)
        if bufv_ref is not None:
            _zero_tails(bufv_ref)

    p = jnp.concatenate([p_ref[r] for r in range(rows)], axis=0)
    (prev,) = _shifted(buf_ref, p_ref, (1,))
    ps = p + (prev - p) * mu_ref[...]
    rec = ps[:, 0:D_RWKV]
    k = ps[:, D_RWKV:2 * D_RWKV]
    v = ps[:, 2 * D_RWKV:3 * D_RWKV]
    wa = ps[:, 3 * D_RWKV:3 * D_RWKV + LANES]
    gc = ps[:, 3 * D_RWKV + LANES:RWKV_COLS]
    logw = -math.exp(-0.5) * _sigmoid(w0_ref[...] + _dot(jnp.tanh(wa), wup_ref[...]))
    a = _sigmoid(a0_ref[...] + _dot(wa, aup_ref[...]))
    g = _dot(_sigmoid(gc), gup_ref[...])
    if has_vres:
        pm = jnp.concatenate([misc_ref[r] for r in range(rows)], axis=0)
        (prevm,) = _shifted(bufv_ref, misc_ref, (1,))
        pmv = pm + (prevm - pm) * muv_ref[...]
        vf = jnp.concatenate([vf_ref[r] for r in range(rows)], axis=0)
        v = v + (vf - v) * _sigmoid(v0_ref[...] + _dot(pmv, vup_ref[...]))
    else:
        for r in range(rows):
            vf_out_ref[r] = v[r * c:(r + 1) * c, :]

    blk = blk_ref[...]
    kkr = k * kk_ref[...]
    kk = kkr * lax.rsqrt(_dot_exact_lhs(kkr * kkr, blk, 2) + L2_EPS)
    k2 = k * (1.0 + (a - 1.0) * ka_ref[...])

    cw = _dot_exact_rhs(tril_ref[...], logw, 3)
    cw_last = _last_row_of_each_chunk(cw, rows, c)
    dec_in = jnp.exp(cw)
    dec_ex = jnp.exp(cw - logw)
    dec_inv = jnp.exp(-cw)
    dec_tail = jnp.exp(cw_last - cw)
    dec_all = jnp.exp(cw_last)
    kka = kk * a
    b_p = (kk * dec_ex).astype(BF16)
    r_p = (rec * dec_in).astype(BF16)
    a_i = (-kka * dec_inv).astype(BF16)
    k_i = (k2 * dec_inv).astype(BF16)
    a_d = (-kka * dec_tail).astype(BF16)
    k_d = (k2 * dec_tail).astype(BF16)
    v_b = v.astype(BF16)

    strict, tril, _ = _pair_masks(c)
    tril2 = jnp.concatenate([tril, tril], axis=1)
    probs = [(n, h, slice(n * c, (n + 1) * c), slice(h * PAIR, (h + 1) * PAIR))
             for n in range(rows) for h in range(H_RWKV // 2)]
    left = [jnp.concatenate([_stack_heads(b_p[rs, sl]), _stack_heads(r_p[rs, sl])], axis=0)
            for _, _, rs, sl in probs]
    v_st = [_stack_heads(v_b[rs, sl]) for _, _, rs, sl in probs]
    m = [_dot_nt(lf, jnp.concatenate([a_i[rs, sl], a_i[rs, sl], k_i[rs, sl], k_i[rs, sl]], axis=0))
         for lf, (_, _, rs, sl) in zip(left, probs)]
    x0 = [_dot_nt(lf, s_ref[n, h]) for lf, (n, h, _, _) in zip(left, probs)]
    u = [xi[0:2 * c] + _dot(jnp.where(strict, mi[0:2 * c, 2 * c:4 * c], 0.0), vi)
         for xi, mi, vi in zip(x0, m, v_st)]
    pw = [jnp.where(strict, mi[0:2 * c, 0:2 * c], 0.0).astype(BF16) for mi in m]
    for j in range(NEUMANN_LEVELS):
        stepped = [_power_step(pi, ui, j == NEUMANN_LEVELS - 1) for pi, ui in zip(pw, u)]
        pw = [sp for sp, _ in stepped]
        u = [su for _, su in stepped]
    uv = [jnp.concatenate([ui.astype(BF16), vi], axis=0) for ui, vi in zip(u, v_st)]
    for i, (n, h, rs, sl) in enumerate(probs):
        y_st = x0[i][2 * c:4 * c] + _dot(jnp.where(tril2, m[i][2 * c:4 * c, :], 0.0), uv[i])
        ys_ref[rs, sl] = y_st[0:c] + y_st[c:2 * c]
    for i, (n, h, rs, sl) in enumerate(probs):
        tails = jnp.concatenate([_stack_heads(a_d[rs, sl]), _stack_heads(k_d[rs, sl])], axis=0)
        s_ref[n, h] = s_ref[n, h] * dec_all[(n + 1) * c - 1:(n + 1) * c, sl] + _dot_tn(uv[i], tails)
    y = ys_ref[...]

    inv_d = 1.0 / HEAD_DIM
    mean = _dot_exact_lhs(y, blk, 2) * inv_d
    yc = y - mean
    var = _dot_exact_lhs(yc * yc, blk, 2) * inv_d
    yn = yc * lax.rsqrt(var + RWKV_GN_EPS) * lnw_ref[...] + lnb_ref[...]
    bonus = _dot_exact_lhs(rec * k2 * rk_ref[...], blk, 2) * v
    out = (yn + bonus) * g
    for r in range(rows):
        y_ref[r] = out[r * c:(r + 1) * c, :]


def _rwkv(p_rwkv, misc, v_first, prm, consts):
    b, t, _ = p_rwkv.shape
    c = CHUNK
    has_vres = v_first is not None
    row = lambda n: _seq_const((1, n))
    common = [prm["mu"], prm["w0"], prm["w_up"], prm["a0"], prm["a_up"], prm["g_up"], prm["k_k"], prm["k_a"],
              prm["r_k"], prm["ln_w"], prm["ln_b"]]
    common_specs = [row(RWKV_COLS), row(D_RWKV), _seq_const((LANES, D_RWKV)), row(D_RWKV),
                    _seq_const((LANES, D_RWKV)), _seq_const((RWKV_GATE_RANK, D_RWKV)), row(D_RWKV), row(D_RWKV),
                    row(D_RWKV), row(D_RWKV), row(D_RWKV)]
    rows = _seq_rows(b)
    const_in = [consts["tril"], consts["blk_rwkv"]]
    const_specs = [_seq_const((rows * c, rows * c)), _seq_const((D_RWKV, D_RWKV))]
    state = pltpu.VMEM((rows, H_RWKV // 2, PAIR, PAIR), F32)
    ys = pltpu.VMEM((rows * c, D_RWKV), F32)
    if has_vres:
        ins = [p_rwkv, misc, v_first] + common + [prm["mu_v"], prm["v0"], prm["v_up"]] + const_in
        in_specs = ([_seq_spec(c, RWKV_COLS, rows), _seq_spec(c, MISC_COLS, rows), _seq_spec(c, D_RWKV, rows)]
                    + common_specs + [row(MISC_COLS), row(D_RWKV), _seq_const((MISC_COLS, D_RWKV))] + const_specs)
        out_specs = _seq_spec(c, D_RWKV, rows)
        out_shape = jax.ShapeDtypeStruct((b, t, D_RWKV), F32)
        scratch = [pltpu.VMEM((rows, SUBLANES + c, RWKV_COLS), F32),
                   pltpu.VMEM((rows, SUBLANES + c, MISC_COLS), F32), state, ys]
    else:
        ins = [p_rwkv] + common + const_in
        in_specs = [_seq_spec(c, RWKV_COLS, rows)] + common_specs + const_specs
        out_specs = [_seq_spec(c, D_RWKV, rows)] * 2
        out_shape = [jax.ShapeDtypeStruct((b, t, D_RWKV), F32)] * 2
        scratch = [pltpu.VMEM((rows, SUBLANES + c, RWKV_COLS), F32), state, ys]
    out = pl.pallas_call(
        functools.partial(_rwkv_body, has_vres),
        grid=(b // rows, t // c),
        in_specs=in_specs, out_specs=out_specs, out_shape=out_shape, scratch_shapes=scratch,
        compiler_params=_seq_params(),
        name="rwkv7_vres" if has_vres else "rwkv7_first",
    )(*ins)
    return (out, v_first) if has_vres else (out[0], out[1])


def _ret_body(p_ref, cos_ref, sin_ref, gnw_ref, dec_ref, qw_ref, kw_ref, cd_ref, blk_ref, y_ref, r_ref):
    t = pl.program_id(1)

    @pl.when(t == 0)
    def _():
        r_ref[...] = jnp.zeros(r_ref.shape, F32)

    p = p_ref[0]
    q = p[:, 0:D_RET]
    k = p[:, D_RET:2 * D_RET]
    v = p[:, 2 * D_RET:3 * D_RET]
    gate = p[:, 3 * D_RET:4 * D_RET]
    cos = jnp.concatenate([cos_ref[0]] * (D_RET // LANES), axis=1)
    sin = jnp.concatenate([sin_ref[0]] * (D_RET // LANES), axis=1)
    lane = lax.broadcasted_iota(jnp.int32, q.shape, 1)
    first_half = (lane % HEAD_DIM) < (HEAD_DIM // 2)
    half = HEAD_DIM // 2

    def rope(x):
        partner = jnp.where(first_half, pltpu.roll(x, D_RET - half, 1), pltpu.roll(x, half, 1))
        return x * cos + partner * sin

    q = rope(q) * HEAD_DIM ** -0.5
    k = rope(k)
    blk = blk_ref[...]
    o = _dot(q * qw_ref[...], r_ref[...])
    for h in range(H_RET):
        in_head = (lane // HEAD_DIM) == h
        scores = _dot_nt(jnp.where(in_head, q, 0.0), k) * dec_ref[h]
        o = o + _dot(scores, jnp.where(in_head, v, 0.0))
    r_ref[...] = r_ref[...] * cd_ref[...] + _dot_tn(k * kw_ref[...], v) * blk.astype(F32)

    inv_d = 1.0 / HEAD_DIM
    mean = _dot_exact_lhs(o, blk, 2) * inv_d
    oc = o - mean
    var = _dot_exact_lhs(oc * oc, blk, 2) * inv_d
    y_ref[0] = _silu(gate) * (oc * lax.rsqrt(var + RET_GN_EPS) * gnw_ref[...])


def _ret(p_ret, cos, sin, gn_w, consts):
    b, t, _ = p_ret.shape
    c = RET_CHUNK
    return pl.pallas_call(
        _ret_body,
        grid=(b, t // c),
        in_specs=[_seq_spec(c, RET_COLS), _seq_spec(c, LANES), _seq_spec(c, LANES), _seq_const((1, D_RET)),
                  _seq_const((H_RET, c, c)), _seq_const((c, D_RET)), _seq_const((c, D_RET)), _seq_const((1, D_RET)),
                  _seq_const((D_RET, D_RET))],
        out_specs=_seq_spec(c, D_RET),
        out_shape=jax.ShapeDtypeStruct((b, t, D_RET), F32),
        scratch_shapes=[pltpu.VMEM((D_RET, D_RET), F32)],
        compiler_params=_seq_params(),
        name="retention",
    )(p_ret, cos, sin, gn_w, consts["ret_decay"], consts["ret_qw"], consts["ret_kw"], consts["ret_cd"],
      consts["blk_ret"])


def _gdn_body(p_ref, misc_ref, convw_ref, alog_ref, dtb_ref, nw_ref, tril_ref, blk_ref, eg_ref, eb_ref, el_ref,
              y_ref, buf_ref, s_ref, os_ref):
    rows, c, _ = p_ref.shape

    @pl.when(pl.program_id(1) == 0)
    def _():
        s_ref[...] = jnp.zeros(s_ref.shape, F32)
        _zero_tails(buf_ref)

    gate = jnp.concatenate([p_ref[r, :, 3 * D_GDN:4 * D_GDN] for r in range(rows)], axis=0)
    taps = _shifted(buf_ref, p_ref, tuple(GDN_CONV - 1 - j for j in range(GDN_CONV)))
    conv = taps[0] * convw_ref[0:1, :]
    for j in range(1, GDN_CONV):
        conv = conv + taps[j] * convw_ref[j:j + 1, :]
    qkv = _silu(conv)
    blk = blk_ref[...]
    q = qkv[:, 0:D_GDN]
    k = qkv[:, D_GDN:2 * D_GDN]
    v = qkv[:, 2 * D_GDN:3 * D_GDN]
    q = q * lax.rsqrt(_dot_exact_lhs(q * q, blk, 2) + L2_EPS) * HEAD_DIM ** -0.5
    k = k * lax.rsqrt(_dot_exact_lhs(k * k, blk, 2) + L2_EPS)

    m = jnp.concatenate([misc_ref[r] for r in range(rows)], axis=0)
    beta = _sigmoid(m)
    g = -jnp.exp(alog_ref[...]) * _softplus(m + dtb_ref[...])
    gcs = _dot_exact_rhs(tril_ref[...], g, 3)
    g_wide = _dot_exact_lhs(gcs, eg_ref[...], 3)
    b_wide = _dot_exact_lhs(beta, eb_ref[...], 2)
    g_last = _dot_exact_lhs(gcs, el_ref[...], 3)

    strict, tril, diag = _pair_masks(c)
    eye = jnp.where(diag, 1.0, 0.0)
    probs = [(n, h, slice(n * c, (n + 1) * c), slice(h * PAIR, (h + 1) * PAIR))
             for n in range(rows) for h in range(H_GDN // 2)]
    gc, be, k_st, kb, qk = [], [], [], [], []
    for n, h, rs, sl in probs:
        w0, w1 = 2 * h * LANES, (2 * h + 1) * LANES
        gc.append(jnp.concatenate([g_wide[rs, w0:w0 + LANES], g_wide[rs, w1:w1 + LANES]], axis=0))
        be.append(jnp.concatenate([b_wide[rs, w0:w0 + LANES], b_wide[rs, w1:w1 + LANES]], axis=0))
        k_st.append(_stack_heads(k[rs, sl]))
        kb.append(k_st[-1] * be[-1])
        qk.append(_dot_nt(jnp.concatenate([kb[-1], _stack_heads(q[rs, sl])], axis=0), k_st[-1]))
    decay = [jnp.where(tril, jnp.exp(jnp.where(tril, g_ - g_.T, 0.0)), 0.0) for g_ in gc]
    tm = [eye - jnp.where(strict, qk_[0:2 * c] * d_, 0.0) for qk_, d_ in zip(qk, decay)]
    pw = [jnp.where(strict, qk_[0:2 * c] * d_, 0.0).astype(BF16) for qk_, d_ in zip(qk, decay)]
    pw = [jnp.dot(p_, p_, preferred_element_type=F32).astype(BF16) for p_ in pw]
    for j in range(1, NEUMANN_LEVELS):
        stepped = [_power_step(p_, t_, j == NEUMANN_LEVELS - 1) for p_, t_ in zip(pw, tm)]
        pw = [sp for sp, _ in stepped]
        tm = [st for _, st in stepped]
    eg = [jnp.exp(g_) for g_ in gc]
    uw = [_dot(t_, jnp.concatenate([_stack_heads(v[rs, sl]) * b_, kb_ * e_], axis=1))
          for t_, b_, kb_, e_, (_, _, rs, sl) in zip(tm, be, kb, eg, probs)]
    v_new = [(uw_[:, 0:LANES] - _dot(uw_[:, LANES:2 * LANES], s_ref[n, h])).astype(BF16)
             for uw_, (n, h, _, _) in zip(uw, probs)]
    for i, (n, h, rs, sl) in enumerate(probs):
        attn = qk[i][2 * c:4 * c] * decay[i]
        o_st = _dot(jnp.concatenate([_stack_heads(q[rs, sl]) * eg[i], attn], axis=1),
                    jnp.concatenate([s_ref[n, h].astype(BF16), v_new[i]], axis=0))
        os_ref[rs, sl] = o_st[0:c] + o_st[c:2 * c]
    for i, (n, h, rs, sl) in enumerate(probs):
        gl = jnp.concatenate([jnp.broadcast_to(gc[i][c - 1:c, :], (c, LANES)),
                              jnp.broadcast_to(gc[i][2 * c - 1:2 * c, :], (c, LANES))], axis=0)
        k_tail = k_st[i] * jnp.exp(gl - gc[i])
        s_ref[n, h] = (s_ref[n, h] * jnp.exp(g_last[(n + 1) * c - 1:(n + 1) * c, sl])
                       + _dot_tn(k_tail, v_new[i]))
    o = os_ref[...]
    ms = _dot_exact_lhs(o * o, blk, 2) * (1.0 / HEAD_DIM)
    out = o * lax.rsqrt(ms + NORM_EPS) * nw_ref[...] * _silu(gate)
    for r in range(rows):
        y_ref[r] = out[r * c:(r + 1) * c, :]


def _gdn(p_gdn, misc, prm, consts):
    b, t, _ = p_gdn.shape
    c = CHUNK
    rows = _seq_rows(b)
    return pl.pallas_call(
        _gdn_body,
        grid=(b // rows, t // c),
        in_specs=[_seq_spec(c, GDN_MAIN_COLS, rows), _seq_spec(c, MISC_COLS, rows),
                  _seq_const((GDN_CONV, 3 * D_GDN)),
                  _seq_const((1, MISC_COLS)), _seq_const((1, MISC_COLS)), _seq_const((1, D_GDN)),
                  _seq_const((rows * c, rows * c)), _seq_const((D_GDN, D_GDN)),
                  _seq_const((MISC_COLS, H_GDN * LANES)),
                  _seq_const((MISC_COLS, H_GDN * LANES)), _seq_const((MISC_COLS, D_GDN))],
        out_specs=_seq_spec(c, D_GDN, rows),
        out_shape=jax.ShapeDtypeStruct((b, t, D_GDN), F32),
        scratch_shapes=[pltpu.VMEM((rows, SUBLANES + c, 3 * D_GDN), F32),
                        pltpu.VMEM((rows, H_GDN // 2, PAIR, PAIR), F32),
                        pltpu.VMEM((rows * c, D_GDN), F32)],
        compiler_params=_seq_params(),
        name="gated_deltanet",
    )(p_gdn, misc, prm["conv_w"], prm["a_log"], prm["dt_bias"], prm["norm_w"], consts["tril"], consts["blk_gdn"],
      consts["exp_g"], consts["exp_b"], consts["exp_l"])


def _block_ones(n):
    i = jnp.arange(n) // HEAD_DIM
    return (i[:, None] == i[None, :]).astype(BF16)


def _constants(rows):
    c = CHUNK
    idx = jnp.arange(rows * c)
    same_chunk = (idx[:, None] // c) == (idx[None, :] // c)
    consts = {"tril": (same_chunk & (idx[:, None] >= idx[None, :])).astype(BF16),
              "blk_rwkv": _block_ones(D_RWKV), "blk_ret": _block_ones(D_RET), "blk_gdn": _block_ones(D_GDN)}
    rc = RET_CHUNK
    ridx = jnp.arange(rc)
    log_gamma = jnp.log(1.0 - 2.0 ** (-5.0 - jnp.arange(H_RET, dtype=F32)))
    rel = ridx[:, None] - ridx[None, :]
    consts["ret_decay"] = jnp.where(rel >= 0, jnp.exp(log_gamma[:, None, None] * jnp.maximum(rel, 0)), 0.0)
    lanes_gamma = jnp.repeat(log_gamma, HEAD_DIM)[None, :]
    consts["ret_qw"] = jnp.exp(lanes_gamma * (ridx[:, None] + 1))
    consts["ret_kw"] = jnp.exp(lanes_gamma * (rc - 1 - ridx[:, None]))
    consts["ret_cd"] = jnp.exp(lanes_gamma * rc)
    rows = jnp.arange(MISC_COLS)[:, None]
    wide_head = jnp.arange(H_GDN * LANES)[None, :] // LANES
    lane_head = jnp.arange(D_GDN)[None, :] // HEAD_DIM
    consts["exp_g"] = (rows == MISC_ALOG + wide_head).astype(BF16)
    consts["exp_b"] = (rows == MISC_BETA + wide_head).astype(BF16)
    consts["exp_l"] = (rows == MISC_ALOG + lane_head).astype(BF16)
    return consts


def _pad_rows(w, top, total):
    return jnp.pad(w, ((top, total - top - w.shape[0]), (0, 0)))


def _pad_lanes(v, left, total):
    return jnp.pad(v, (left, total - left - v.shape[0]))[None, :]


def kernel(x, positions, norm_g, ffn_w_gate, ffn_w_up, ffn_w_down, w_in_first, w_in_rest, w_out, rwkv_mu, rwkv_w0, rwkv_w_up, rwkv_a0, rwkv_a_up, rwkv_g_up, rwkv_k_k, rwkv_k_a, rwkv_r_k, rwkv_ln_w, rwkv_ln_b, rwkv_mu_vres, rwkv_v0, rwkv_v_up, ret_gn_w, gdn_conv_w, gdn_A_log, gdn_dt_bias, gdn_norm_w):
    b, t, d = x.shape
    depth = norm_g.shape[0]
    n = b * t
    assert d == D_MODEL and n % ROW_TILE == 0 and t % RET_CHUNK == 0 and t % CHUNK == 0
    consts = _constants(_seq_rows(b))
    cos, sin = _rope_tables(positions)
    x = x.reshape(n, d)
    v_first = None
    for l in range(depth):
        ng = norm_g[l][:, None, :]
        wg, wu, wd = (w[l].astype(BF16) for w in (ffn_w_gate, ffn_w_up, ffn_w_down))
        x = _ffn(x, ng[0], ng[1], wg[0], wu[0], wd[0])

        w_in = w_in_first if l == 0 else w_in_rest[l - 1]
        gdn_small = w_in[:, RWKV_COLS + RET_COLS + GDN_MAIN_COLS:IN_COLS_FIRST]
        vres_cols = w_in[:, IN_COLS_FIRST:] if l > 0 else jnp.zeros((d, RWKV_VRES_RANK), F32)
        misc_w = jnp.concatenate([gdn_small, vres_cols], axis=1)
        misc_w = jnp.pad(misc_w, ((0, 0), (0, MISC_COLS - misc_w.shape[1])))
        w_packed = jnp.concatenate([w_in[:, :RWKV_COLS + RET_COLS + GDN_MAIN_COLS], misc_w], axis=1).astype(BF16)
        p_rwkv, p_ret, p_gdn, p_misc = _inproj(x, ng[2], w_packed)
        p_rwkv = p_rwkv.reshape(b, t, RWKV_COLS)
        p_ret = p_ret.reshape(b, t, RET_COLS)
        p_gdn = p_gdn.reshape(b, t, GDN_MAIN_COLS)
        p_misc = p_misc.reshape(b, t, MISC_COLS)

        rw = {"mu": rwkv_mu[l][None, :], "w0": rwkv_w0[l][None, :],
              "w_up": _pad_rows(rwkv_w_up[l], 0, LANES).astype(BF16), "a0": rwkv_a0[l][None, :],
              "a_up": _pad_rows(rwkv_a_up[l], RWKV_DECAY_RANK, LANES).astype(BF16),
              "g_up": rwkv_g_up[l].astype(BF16), "k_k": rwkv_k_k[l][None, :], "k_a": rwkv_k_a[l][None, :],
              "r_k": rwkv_r_k[l][None, :], "ln_w": rwkv_ln_w[l][None, :], "ln_b": rwkv_ln_b[l][None, :]}
        if l > 0:
            rw["mu_v"] = _pad_lanes(rwkv_mu_vres[l - 1], MISC_VRES, MISC_COLS)
            rw["v0"] = rwkv_v0[l - 1][None, :]
            rw["v_up"] = _pad_rows(rwkv_v_up[l - 1], MISC_VRES, MISC_COLS).astype(BF16)
        y_a, v_first = _rwkv(p_rwkv, p_misc, v_first, rw, consts)
        y_b = _ret(p_ret, cos, sin, ret_gn_w[l][None, :], consts)
        gd = {"conv_w": gdn_conv_w[l], "a_log": _pad_lanes(gdn_A_log[l], MISC_ALOG, MISC_COLS),
              "dt_bias": _pad_lanes(gdn_dt_bias[l], MISC_ALOG, MISC_COLS),
              "norm_w": jnp.tile(gdn_norm_w[l], H_GDN)[None, :]}
        y_c = _gdn(p_gdn, p_misc, gd, consts)

        wo = w_out[l].astype(BF16)
        x = _outproj(x, y_a.reshape(n, D_RWKV), y_b.reshape(n, D_RET), y_c.reshape(n, D_GDN),
                     wo[:D_RWKV], wo[D_RWKV:D_RWKV + D_RET], wo[D_RWKV + D_RET:], ng[3])
        x = _ffn(x, ng[4], ng[5], wg[1], wu[1], wd[1])
    return x.reshape(b, t, d)
```

```python
import functools
import math

import jax
import jax.numpy as jnp
from jax import lax
from jax.experimental import pallas as pl
from jax.experimental.pallas import tpu as pltpu

F32 = jnp.float32
BF16 = jnp.bfloat16

D_MODEL = 1024
D_FF = 2816
HEAD_DIM = 64
H_RWKV, H_RET, H_GDN = 6, 4, 6
D_RWKV, D_RET, D_GDN = H_RWKV * HEAD_DIM, H_RET * HEAD_DIM, H_GDN * HEAD_DIM
RWKV_DECAY_RANK, RWKV_ICLR_RANK, RWKV_VRES_RANK, RWKV_GATE_RANK = 64, 64, 32, 128
RWKV_COLS = 3 * D_RWKV + RWKV_DECAY_RANK + RWKV_ICLR_RANK + RWKV_GATE_RANK
RET_COLS = 4 * D_RET
GDN_MAIN_COLS = 4 * D_GDN
GDN_COLS = GDN_MAIN_COLS + 2 * H_GDN
IN_COLS_FIRST = RWKV_COLS + RET_COLS + GDN_COLS
GDN_CONV = 4
NORM_EPS = 1e-6
L2_EPS = 1e-6
RWKV_GN_EPS = HEAD_DIM * 1e-5
RET_GN_EPS = 1e-5
ROPE_THETA = 10000.0

LANES = 128
SUBLANES = 8
MISC_COLS = LANES
MISC_BETA, MISC_ALOG, MISC_VRES = 0, H_GDN, 2 * H_GDN
CHUNK = 64
RET_CHUNK = 128
PAIR = 2 * HEAD_DIM
SEQ_ROWS = 4
ROW_TILE = 512
FF_CHUNK = 512
SUM_PARTS = 3
NORM_PARTS = 2
VMEM_LIMIT = 56 * 1024 * 1024


def _dot(a, b):
    return jnp.dot(a.astype(BF16), b.astype(BF16), preferred_element_type=F32)


def _dot_nt(a, b):
    return lax.dot_general(a.astype(BF16), b.astype(BF16), (((1,), (1,)), ((), ())),
                           preferred_element_type=F32)


def _dot_tn(a, b):
    return lax.dot_general(a.astype(BF16), b.astype(BF16), (((0,), (0,)), ((), ())),
                           preferred_element_type=F32)


def _split(x, parts):
    out = []
    for _ in range(parts - 1):
        h = x.astype(BF16)
        out.append(h)
        x = x - h.astype(F32)
    out.append(x.astype(BF16))
    return out


def _dot_exact_rhs(a_tiled, x):
    parts = a_tiled.shape[1] // x.shape[0]
    return jnp.dot(a_tiled, jnp.concatenate(_split(x, parts), axis=0), preferred_element_type=F32)


def _dot_exact_lhs(x, b_tiled):
    parts = b_tiled.shape[0] // x.shape[1]
    return jnp.dot(jnp.concatenate(_split(x, parts), axis=1), b_tiled, preferred_element_type=F32)


def _rms(x, g):
    return x * lax.rsqrt(jnp.mean(x * x, axis=-1, keepdims=True) + NORM_EPS) * g


def _sigmoid(x):
    return 1.0 / (1.0 + jnp.exp(-x))


def _silu(x):
    return x * _sigmoid(x)


def _softplus(x):
    return jnp.maximum(x, 0.0) + jnp.log(1.0 + jnp.exp(-jnp.abs(x)))


def _stack_heads(x):
    lane = lax.broadcasted_iota(jnp.int32, x.shape, 1)
    return jnp.concatenate([jnp.where(lane < HEAD_DIM, x, 0.0), jnp.where(lane >= HEAD_DIM, x, 0.0)], axis=0)


def _pair_masks(c):
    i = lax.broadcasted_iota(jnp.int32, (2 * c, 2 * c), 0)
    j = lax.broadcasted_iota(jnp.int32, (2 * c, 2 * c), 1)
    same = ((i < c) & (j < c)) | ((i >= c) & (j >= c))
    return same & (i > j), same & (i >= j), (i == j)


NEUMANN_LEVELS = int(math.log2(CHUNK))


def _power_step(p, x, last):
    n = p.shape[1]
    if last:
        return None, x + jnp.dot(p, x.astype(BF16), preferred_element_type=F32)
    z = jnp.dot(p, jnp.concatenate([p, x.astype(BF16)], axis=1), preferred_element_type=F32)
    return z[:, :n].astype(BF16), x + z[:, n:]


def _ffn_body(x_ref, gpre_ref, gpost_ref, wg_ref, wu_ref, wd_ref, o_ref, h_ref, a_ref):
    x = x_ref[...]
    h_ref[...] = _rms(x, gpre_ref[...]).astype(BF16)
    for c0 in range(0, D_FF, FF_CHUNK):
        w = min(FF_CHUNK, D_FF - c0)
        h = h_ref[...]
        g = jnp.dot(h, wg_ref[:, c0:c0 + w], preferred_element_type=F32)
        u = jnp.dot(h, wu_ref[:, c0:c0 + w], preferred_element_type=F32)
        a_ref[:, c0:c0 + w] = (_silu(g) * u).astype(BF16)
    y = jnp.dot(a_ref[...], wd_ref[...], preferred_element_type=F32)
    o_ref[...] = x + 0.5 * _rms(y, gpost_ref[...])


def _row_spec(width):
    return pl.BlockSpec((ROW_TILE, width), lambda i: (i, 0))


def _const_spec(shape):
    zeros = (0,) * len(shape)
    return pl.BlockSpec(shape, lambda *_: zeros, pipeline_mode=pl.Buffered(1))


def _dense_params():
    return pltpu.CompilerParams(dimension_semantics=("parallel",), vmem_limit_bytes=VMEM_LIMIT)


def _layer_spec(shape, *lead):
    return pl.BlockSpec((None,) * len(lead) + shape, lambda *_: lead + (0,) * len(shape),
                        pipeline_mode=pl.Buffered(1))


def _ffn(x, g_pre, g_post, wg, wu, wd, layer, half):
    n = x.shape[0]
    return pl.pallas_call(
        _ffn_body,
        grid=(n // ROW_TILE,),
        in_specs=[_row_spec(D_MODEL), _const_spec((1, D_MODEL)), _const_spec((1, D_MODEL)),
                  _layer_spec((D_MODEL, D_FF), layer, half), _layer_spec((D_MODEL, D_FF), layer, half),
                  _layer_spec((D_FF, D_MODEL), layer, half)],
        out_specs=_row_spec(D_MODEL),
        out_shape=jax.ShapeDtypeStruct((n, D_MODEL), F32),
        scratch_shapes=[pltpu.VMEM((ROW_TILE, D_MODEL), BF16), pltpu.VMEM((ROW_TILE, D_FF), BF16)],
        compiler_params=_dense_params(),
        name="ffn",
    )(x, g_pre, g_post, wg, wu, wd)


_IN_WIDTHS = (RWKV_COLS, RET_COLS, GDN_MAIN_COLS, MISC_COLS)
IN_COLS_PACKED = sum(_IN_WIDTHS)


def _inproj_body(x_ref, g_ref, w_ref, o_rwkv, o_ret, o_gdn, o_misc, h_ref):
    h_ref[...] = _rms(x_ref[...], g_ref[...]).astype(BF16)
    off = 0
    for o_ref in (o_rwkv, o_ret, o_gdn, o_misc):
        width = o_ref.shape[-1]
        for c0 in range(0, width, FF_CHUNK):
            w = min(FF_CHUNK, width - c0)
            o_ref[:, c0:c0 + w] = jnp.dot(h_ref[...], w_ref[:, off + c0:off + c0 + w],
                                          preferred_element_type=F32)
        off += width


def _inproj(x, g, w):
    n = x.shape[0]
    return pl.pallas_call(
        _inproj_body,
        grid=(n // ROW_TILE,),
        in_specs=[_row_spec(D_MODEL), _const_spec((1, D_MODEL)), _const_spec((D_MODEL, IN_COLS_PACKED))],
        out_specs=[_row_spec(w_) for w_ in _IN_WIDTHS],
        out_shape=[jax.ShapeDtypeStruct((n, w_), F32) for w_ in _IN_WIDTHS],
        scratch_shapes=[pltpu.VMEM((ROW_TILE, D_MODEL), BF16)],
        compiler_params=_dense_params(),
        name="inproj",
    )(x, g, w)


def _outproj_body(x_ref, ya_ref, yb_ref, yc_ref, w_ref, g_ref, o_ref):
    y_all = jnp.concatenate([ya_ref[...].astype(BF16), yb_ref[...].astype(BF16), yc_ref[...].astype(BF16)], axis=1)
    y = jnp.dot(y_all, w_ref[...], preferred_element_type=F32)
    o_ref[...] = x_ref[...] + _rms(y, g_ref[...])


def _outproj(x, ya, yb, yc, w, g, layer):
    n = x.shape[0]
    return pl.pallas_call(
        _outproj_body,
        grid=(n // ROW_TILE,),
        in_specs=[_row_spec(D_MODEL), _row_spec(D_RWKV), _row_spec(D_RET), _row_spec(D_GDN),
                  _layer_spec((D_RWKV + D_RET + D_GDN, D_MODEL), layer), _const_spec((1, D_MODEL))],
        out_specs=_row_spec(D_MODEL),
        out_shape=jax.ShapeDtypeStruct((n, D_MODEL), F32),
        compiler_params=_dense_params(),
        name="outproj",
    )(x, ya, yb, yc, w, g)


def _rope_body(pos_ref, invf_ref, sign_ref, cos_ref, sin_ref):
    ang = pos_ref[0] * invf_ref[...]
    cos_ref[0] = jnp.cos(ang)
    sin_ref[0] = jnp.sin(ang) * sign_ref[...]


def _rope_tables(positions):
    b, t = positions.shape
    half = HEAD_DIM // 2
    inv_freq = ROPE_THETA ** (-jnp.arange(0, HEAD_DIM, 2, dtype=F32) / HEAD_DIM)
    invf = jnp.tile(inv_freq, LANES // half)[None, :]
    sign = jnp.tile(jnp.concatenate([-jnp.ones((half,), F32), jnp.ones((half,), F32)]), LANES // HEAD_DIM)[None, :]
    pos = positions.astype(F32)[..., None]
    tt = min(t, 512)
    return pl.pallas_call(
        _rope_body,
        grid=(b, t // tt),
        in_specs=[pl.BlockSpec((1, tt, 1), lambda i, j: (i, j, 0)),
                  pl.BlockSpec((1, LANES), lambda i, j: (0, 0)),
                  pl.BlockSpec((1, LANES), lambda i, j: (0, 0))],
        out_specs=[pl.BlockSpec((1, tt, LANES), lambda i, j: (i, j, 0))] * 2,
        out_shape=[jax.ShapeDtypeStruct((b, t, LANES), F32)] * 2,
        compiler_params=pltpu.CompilerParams(dimension_semantics=("parallel", "parallel")),
        name="rope_tables",
    )(pos, invf, sign)


def _seq_spec(c, width, rows=1):
    return pl.BlockSpec((rows, c, width), lambda b, t: (b, t, 0))


def _seq_const(shape):
    zeros = (0,) * len(shape)
    return pl.BlockSpec(shape, lambda b, t: zeros)


def _seq_rows(b):
    return math.gcd(b, SEQ_ROWS)


def _seq_params():
    return pltpu.CompilerParams(dimension_semantics=("parallel", "arbitrary"), vmem_limit_bytes=VMEM_LIMIT)


def _shifted(buf_ref, x_ref, shifts):
    rows, c, _ = x_ref.shape
    width = buf_ref.shape[-1]
    out = [[] for _ in shifts]
    for r in range(rows):
        x = x_ref[r, :, 0:width]
        buf_ref[r, SUBLANES:SUBLANES + c, :] = x
        for i, s in enumerate(shifts):
            out[i].append(buf_ref[r, SUBLANES - s:SUBLANES - s + c, :])
        buf_ref[r, 0:SUBLANES, :] = x[c - SUBLANES:c, :]
    return [jnp.concatenate(o, axis=0) for o in out]


def _zero_tails(buf_ref):
    rows, _, width = buf_ref.shape
    buf_ref[:, 0:SUBLANES, :] = jnp.zeros((rows, SUBLANES, width), F32)


def _last_row_of_each_chunk(x, rows, c):
    return jnp.concatenate([jnp.broadcast_to(x[(r + 1) * c - 1:(r + 1) * c, :], (c, x.shape[1]))
                            for r in range(rows)], axis=0)


def _rwkv_body(has_vres, *refs):
    if has_vres:
        (p_ref, misc_ref, vf_ref, mu_ref, w0_ref, wup_ref, a0_ref, aup_ref, gup_ref, kk_ref, ka_ref, rk_ref,
         lnw_ref, lnb_ref, muv_ref, v0_ref, vup_ref, tril_ref, blk_ref,
         y_ref, buf_ref, bufv_ref, s_ref, ys_ref) = refs
    else:
        (p_ref, mu_ref, w0_ref, wup_ref, a0_ref, aup_ref, gup_ref, kk_ref, ka_ref, rk_ref,
         lnw_ref, lnb_ref, tril_ref, blk_ref,
         y_ref, vf_out_ref, buf_ref, s_ref, ys_ref) = refs
        bufv_ref = None
    rows, c, _ = p_ref.shape

    @pl.when(pl.program_id(1) == 0)
    def _():
        s_ref[...] = jnp.zeros(s_ref.shape, F32)
        _zero_tails(buf_ref)
        if bufv_ref is not None:
            _zero_tails(bufv_ref)

    p = jnp.concatenate([p_ref[r] for r in range(rows)], axis=0)
    (prev,) = _shifted(buf_ref, p_ref, (1,))
    ps = p + (prev - p) * mu_ref[...]
    rec = ps[:, 0:D_RWKV]
    k = ps[:, D_RWKV:2 * D_RWKV]
    v = ps[:, 2 * D_RWKV:3 * D_RWKV]
    wa = ps[:, 3 * D_RWKV:3 * D_RWKV + LANES]
    gc = ps[:, 3 * D_RWKV + LANES:RWKV_COLS]
    logw = -math.exp(-0.5) * _sigmoid(w0_ref[...] + _dot(jnp.tanh(wa), wup_ref[...]))
    a = _sigmoid(a0_ref[...] + _dot(wa, aup_ref[...]))
    g = _dot(_sigmoid(gc), gup_ref[...])
    if has_vres:
        pm = jnp.concatenate([misc_ref[r] for r in range(rows)], axis=0)
        (prevm,) = _shifted(bufv_ref, misc_ref, (1,))
        pmv = pm + (prevm - pm) * muv_ref[...]
        vf = jnp.concatenate([vf_ref[r] for r in range(rows)], axis=0)
        v = v + (vf - v) * _sigmoid(v0_ref[...] + _dot(pmv, vup_ref[...]))
    else:
        for r in range(rows):
            vf_out_ref[r] = v[r * c:(r + 1) * c, :]

    blk = blk_ref[...]
    kkr = k * kk_ref[...]
    kk = kkr * lax.rsqrt(_dot_exact_lhs(kkr * kkr, blk) + L2_EPS)
    k2 = k * (1.0 + (a - 1.0) * ka_ref[...])

    cw = _dot_exact_rhs(tril_ref[...], logw)
    cw_last = _last_row_of_each_chunk(cw, rows, c)
    dec_in = jnp.exp(cw)
    dec_ex = jnp.exp(cw - logw)
    dec_inv = jnp.exp(-cw)
    dec_tail = jnp.exp(cw_last - cw)
    dec_all = jnp.exp(cw_last)
    kka = kk * a
    b_p = (kk * dec_ex).astype(BF16)
    r_p = (rec * dec_in).astype(BF16)
    a_i = (-kka * dec_inv).astype(BF16)
    k_i = (k2 * dec_inv).astype(BF16)
    a_d = (-kka * dec_tail).astype(BF16)
    k_d = (k2 * dec_tail).astype(BF16)
    v_b = v.astype(BF16)

    strict, tril, _ = _pair_masks(c)
    tril2 = jnp.concatenate([tril, tril], axis=1)
    probs = [(n, h, slice(n * c, (n + 1) * c), slice(h * PAIR, (h + 1) * PAIR))
             for n in range(rows) for h in range(H_RWKV // 2)]
    left = [jnp.concatenate([_stack_heads(b_p[rs, sl]), _stack_heads(r_p[rs, sl])], axis=0)
            for _, _, rs, sl in probs]
    v_st = [_stack_heads(v_b[rs, sl]) for _, _, rs, sl in probs]
    m = [_dot_nt(lf, jnp.concatenate([a_i[rs, sl], a_i[rs, sl], k_i[rs, sl], k_i[rs, sl]], axis=0))
         for lf, (_, _, rs, sl) in zip(left, probs)]
    x0 = [_dot_nt(lf, s_ref[n, h]) for lf, (n, h, _, _) in zip(left, probs)]
    u = [xi[0:2 * c] + _dot(jnp.where(strict, mi[0:2 * c, 2 * c:4 * c], 0.0), vi)
         for xi, mi, vi in zip(x0, m, v_st)]
    pw = [jnp.where(strict, mi[0:2 * c, 0:2 * c], 0.0).astype(BF16) for mi in m]
    for j in range(NEUMANN_LEVELS):
        stepped = [_power_step(pi, ui, j == NEUMANN_LEVELS - 1) for pi, ui in zip(pw, u)]
        pw = [sp for sp, _ in stepped]
        u = [su for _, su in stepped]
    uv = [jnp.concatenate([ui.astype(BF16), vi], axis=0) for ui, vi in zip(u, v_st)]
    for i, (n, h, rs, sl) in enumerate(probs):
        y_st = x0[i][2 * c:4 * c] + _dot(jnp.where(tril2, m[i][2 * c:4 * c, :], 0.0), uv[i])
        ys_ref[rs, sl] = y_st[0:c] + y_st[c:2 * c]
    for i, (n, h, rs, sl) in enumerate(probs):
        tails = jnp.concatenate([_stack_heads(a_d[rs, sl]), _stack_heads(k_d[rs, sl])], axis=0)
        s_ref[n, h] = s_ref[n, h] * dec_all[(n + 1) * c - 1:(n + 1) * c, sl] + _dot_tn(uv[i], tails)
    y = ys_ref[...]

    inv_d = 1.0 / HEAD_DIM
    mean = _dot_exact_lhs(y, blk) * inv_d
    yc = y - mean
    var = _dot_exact_lhs(yc * yc, blk) * inv_d
    yn = yc * lax.rsqrt(var + RWKV_GN_EPS) * lnw_ref[...] + lnb_ref[...]
    bonus = _dot_exact_lhs(rec * k2 * rk_ref[...], blk) * v
    out = (yn + bonus) * g
    for r in range(rows):
        y_ref[r] = out[r * c:(r + 1) * c, :]


def _rwkv(p_rwkv, misc, v_first, prm, consts):
    b, t, _ = p_rwkv.shape
    c = CHUNK
    has_vres = v_first is not None
    row = lambda n: _seq_const((1, n))
    common = [prm["mu"], prm["w0"], prm["w_up"], prm["a0"], prm["a_up"], prm["g_up"], prm["k_k"], prm["k_a"],
              prm["r_k"], prm["ln_w"], prm["ln_b"]]
    common_specs = [row(RWKV_COLS), row(D_RWKV), _seq_const((LANES, D_RWKV)), row(D_RWKV),
                    _seq_const((LANES, D_RWKV)), _seq_const((RWKV_GATE_RANK, D_RWKV)), row(D_RWKV), row(D_RWKV),
                    row(D_RWKV), row(D_RWKV), row(D_RWKV)]
    rows = _seq_rows(b)
    const_in = [consts["tril"], consts["blk_rwkv"]]
    const_specs = [_seq_const(a.shape) for a in const_in]
    state = pltpu.VMEM((rows, H_RWKV // 2, PAIR, PAIR), F32)
    ys = pltpu.VMEM((rows * c, D_RWKV), F32)
    if has_vres:
        ins = [p_rwkv, misc, v_first] + common + [prm["mu_v"], prm["v0"], prm["v_up"]] + const_in
        in_specs = ([_seq_spec(c, RWKV_COLS, rows), _seq_spec(c, MISC_COLS, rows), _seq_spec(c, D_RWKV, rows)]
                    + common_specs + [row(MISC_COLS), row(D_RWKV), _seq_const((MISC_COLS, D_RWKV))] + const_specs)
        out_specs = _seq_spec(c, D_RWKV, rows)
        out_shape = jax.ShapeDtypeStruct((b, t, D_RWKV), F32)
        scratch = [pltpu.VMEM((rows, SUBLANES + c, RWKV_COLS), F32),
                   pltpu.VMEM((rows, SUBLANES + c, MISC_COLS), F32), state, ys]
    else:
        ins = [p_rwkv] + common + const_in
        in_specs = [_seq_spec(c, RWKV_COLS, rows)] + common_specs + const_specs
        out_specs = [_seq_spec(c, D_RWKV, rows)] * 2
        out_shape = [jax.ShapeDtypeStruct((b, t, D_RWKV), F32)] * 2
        scratch = [pltpu.VMEM((rows, SUBLANES + c, RWKV_COLS), F32), state, ys]
    out = pl.pallas_call(
        functools.partial(_rwkv_body, has_vres),
        grid=(b // rows, t // c),
        in_specs=in_specs, out_specs=out_specs, out_shape=out_shape, scratch_shapes=scratch,
        compiler_params=_seq_params(),
        name="rwkv7_vres" if has_vres else "rwkv7_first",
    )(*ins)
    return (out, v_first) if has_vres else (out[0], out[1])


def _ret_body(p_ref, cos_ref, sin_ref, gnw_ref, dec_ref, qw_ref, kw_ref, cd_ref, blk_ref, mask_ref, y_ref, r_ref):
    rows, c, _ = p_ref.shape

    @pl.when(pl.program_id(1) == 0)
    def _():
        r_ref[...] = jnp.zeros(r_ref.shape, F32)

    stack = lambda ref, reps: jnp.concatenate(
        [jnp.concatenate([ref[n]] * reps, axis=1) for n in range(rows)], axis=0)
    p = stack(p_ref, 1)
    q = p[:, 0:D_RET]
    k = p[:, D_RET:2 * D_RET]
    v = p[:, 2 * D_RET:3 * D_RET].astype(BF16)
    gate = p[:, 3 * D_RET:4 * D_RET]
    cos = stack(cos_ref, D_RET // LANES)
    sin = stack(sin_ref, D_RET // LANES)
    lane = lax.broadcasted_iota(jnp.int32, q.shape, 1)
    first_half = (lane % HEAD_DIM) < (HEAD_DIM // 2)
    half = HEAD_DIM // 2

    def rope(x):
        partner = jnp.where(first_half, pltpu.roll(x, D_RET - half, 1), pltpu.roll(x, half, 1))
        return x * cos + partner * sin

    q = rope(q) * HEAD_DIM ** -0.5
    k = rope(k)
    q_in = (q * jnp.concatenate([qw_ref[...]] * rows, axis=0)).astype(BF16)
    k_out = (k * jnp.concatenate([kw_ref[...]] * rows, axis=0)).astype(BF16)
    q = q.astype(BF16)
    k = k.astype(BF16)
    chunk_lane = lax.broadcasted_iota(jnp.int32, (c, D_RET), 1)
    in_head = [(chunk_lane // HEAD_DIM) == h for h in range(H_RET)]
    chunks = [(n, slice(n * c, (n + 1) * c)) for n in range(rows)]
    scores = [[(_dot_nt(jnp.where(m, q[rs], 0.0), k[rs]) * dec_ref[h]).astype(BF16) for h, m in enumerate(in_head)]
              for _, rs in chunks]
    outs = []
    for (n, rs), sc in zip(chunks, scores):
        lhs = jnp.concatenate(sc + [q_in[rs]], axis=1)
        rhs = jnp.concatenate([jnp.where(m, v[rs], 0.0) for m in in_head] + [r_ref[n].astype(BF16)], axis=0)
        outs.append(jnp.dot(lhs, rhs, preferred_element_type=F32))
    for n, rs in chunks:
        r_ref[n] = r_ref[n] * cd_ref[...] + _dot_tn(k_out[rs], v[rs]) * mask_ref[...].astype(F32)
    o = jnp.concatenate(outs, axis=0)

    blk = blk_ref[...]
    inv_d = 1.0 / HEAD_DIM
    mean = _dot_exact_lhs(o, blk) * inv_d
    oc = o - mean
    var = _dot_exact_lhs(oc * oc, blk) * inv_d
    out = _silu(gate) * (oc * lax.rsqrt(var + RET_GN_EPS) * gnw_ref[...])
    for n, rs in chunks:
        y_ref[n] = out[rs]


def _ret(p_ret, cos, sin, gn_w, consts):
    b, t, _ = p_ret.shape
    c = RET_CHUNK
    rows = _seq_rows(b)
    tables = [consts[name] for name in ("ret_decay", "ret_qw", "ret_kw", "ret_cd", "blk_ret", "ret_mask")]
    return pl.pallas_call(
        _ret_body,
        grid=(b // rows, t // c),
        in_specs=[_seq_spec(c, RET_COLS, rows), _seq_spec(c, LANES, rows), _seq_spec(c, LANES, rows),
                  _seq_const((1, D_RET))] + [_seq_const(a.shape) for a in tables],
        out_specs=_seq_spec(c, D_RET, rows),
        out_shape=jax.ShapeDtypeStruct((b, t, D_RET), F32),
        scratch_shapes=[pltpu.VMEM((rows, D_RET, D_RET), F32)],
        compiler_params=_seq_params(),
        name="retention",
    )(p_ret, cos, sin, gn_w, *tables)


def _gdn_body(p_ref, misc_ref, convw_ref, alog_ref, dtb_ref, nw_ref, tril_ref, blk_ref, eg_ref, eb_ref, el_ref,
              y_ref, buf_ref, s_ref, os_ref):
    rows, c, _ = p_ref.shape

    @pl.when(pl.program_id(1) == 0)
    def _():
        s_ref[...] = jnp.zeros(s_ref.shape, F32)
        _zero_tails(buf_ref)

    gate = jnp.concatenate([p_ref[r, :, 3 * D_GDN:4 * D_GDN] for r in range(rows)], axis=0)
    taps = _shifted(buf_ref, p_ref, tuple(GDN_CONV - 1 - j for j in range(GDN_CONV)))
    conv = taps[0] * convw_ref[0:1, :]
    for j in range(1, GDN_CONV):
        conv = conv + taps[j] * convw_ref[j:j + 1, :]
    qkv = _silu(conv)
    blk = blk_ref[...]
    q = qkv[:, 0:D_GDN]
    k = qkv[:, D_GDN:2 * D_GDN]
    v = qkv[:, 2 * D_GDN:3 * D_GDN]
    q = q * lax.rsqrt(_dot_exact_lhs(q * q, blk) + L2_EPS) * HEAD_DIM ** -0.5
    k = k * lax.rsqrt(_dot_exact_lhs(k * k, blk) + L2_EPS)

    m = jnp.concatenate([misc_ref[r] for r in range(rows)], axis=0)
    beta = _sigmoid(m)
    g = -jnp.exp(alog_ref[...]) * _softplus(m + dtb_ref[...])
    gcs = _dot_exact_rhs(tril_ref[...], g)
    g_wide = _dot_exact_lhs(gcs, eg_ref[...])
    b_wide = _dot_exact_lhs(beta, eb_ref[...])
    g_last = _dot_exact_lhs(gcs, el_ref[...])

    strict, tril, diag = _pair_masks(c)
    eye = jnp.where(diag, 1.0, 0.0)
    probs = [(n, h, slice(n * c, (n + 1) * c), slice(h * PAIR, (h + 1) * PAIR))
             for n in range(rows) for h in range(H_GDN // 2)]
    gc, be, k_st, kb, qk = [], [], [], [], []
    for n, h, rs, sl in probs:
        w0, w1 = 2 * h * LANES, (2 * h + 1) * LANES
        gc.append(jnp.concatenate([g_wide[rs, w0:w0 + LANES], g_wide[rs, w1:w1 + LANES]], axis=0))
        be.append(jnp.concatenate([b_wide[rs, w0:w0 + LANES], b_wide[rs, w1:w1 + LANES]], axis=0))
        k_st.append(_stack_heads(k[rs, sl]))
        kb.append(k_st[-1] * be[-1])
        qk.append(_dot_nt(jnp.concatenate([kb[-1], _stack_heads(q[rs, sl])], axis=0), k_st[-1]))
    decay = [jnp.where(tril, jnp.exp(jnp.where(tril, g_ - g_.T, 0.0)), 0.0) for g_ in gc]
    tm = [eye - jnp.where(strict, qk_[0:2 * c] * d_, 0.0) for qk_, d_ in zip(qk, decay)]
    pw = [jnp.where(strict, qk_[0:2 * c] * d_, 0.0).astype(BF16) for qk_, d_ in zip(qk, decay)]
    pw = [jnp.dot(p_, p_, preferred_element_type=F32).astype(BF16) for p_ in pw]
    for j in range(1, NEUMANN_LEVELS):
        stepped = [_power_step(p_, t_, j == NEUMANN_LEVELS - 1) for p_, t_ in zip(pw, tm)]
        pw = [sp for sp, _ in stepped]
        tm = [st for _, st in stepped]
    eg = [jnp.exp(g_) for g_ in gc]
    uw = [_dot(t_, jnp.concatenate([_stack_heads(v[rs, sl]) * b_, kb_ * e_], axis=1))
          for t_, b_, kb_, e_, (_, _, rs, sl) in zip(tm, be, kb, eg, probs)]
    v_new = [(uw_[:, 0:LANES] - _dot(uw_[:, LANES:2 * LANES], s_ref[n, h])).astype(BF16)
             for uw_, (n, h, _, _) in zip(uw, probs)]
    for i, (n, h, rs, sl) in enumerate(probs):
        attn = qk[i][2 * c:4 * c] * decay[i]
        o_st = _dot(jnp.concatenate([_stack_heads(q[rs, sl]) * eg[i], attn], axis=1),
                    jnp.concatenate([s_ref[n, h].astype(BF16), v_new[i]], axis=0))
        os_ref[rs, sl] = o_st[0:c] + o_st[c:2 * c]
    for i, (n, h, rs, sl) in enumerate(probs):
        gl = jnp.concatenate([jnp.broadcast_to(gc[i][c - 1:c, :], (c, LANES)),
                              jnp.broadcast_to(gc[i][2 * c - 1:2 * c, :], (c, LANES))], axis=0)
        k_tail = k_st[i] * jnp.exp(gl - gc[i])
        s_ref[n, h] = (s_ref[n, h] * jnp.exp(g_last[(n + 1) * c - 1:(n + 1) * c, sl])
                       + _dot_tn(k_tail, v_new[i]))
    o = os_ref[...]
    ms = _dot_exact_lhs(o * o, blk) * (1.0 / HEAD_DIM)
    out = o * lax.rsqrt(ms + NORM_EPS) * nw_ref[...] * _silu(gate)
    for r in range(rows):
        y_ref[r] = out[r * c:(r + 1) * c, :]


def _gdn(p_gdn, misc, prm, consts):
    b, t, _ = p_gdn.shape
    c = CHUNK
    rows = _seq_rows(b)
    return pl.pallas_call(
        _gdn_body,
        grid=(b // rows, t // c),
        in_specs=[_seq_spec(c, GDN_MAIN_COLS, rows), _seq_spec(c, MISC_COLS, rows),
                  _seq_const((GDN_CONV, 3 * D_GDN)),
                  _seq_const((1, MISC_COLS)), _seq_const((1, MISC_COLS)), _seq_const((1, D_GDN)),
                  ] + [_seq_const(consts[name].shape) for name in ("tril", "blk_gdn", "exp_g", "exp_b", "exp_l")],
        out_specs=_seq_spec(c, D_GDN, rows),
        out_shape=jax.ShapeDtypeStruct((b, t, D_GDN), F32),
        scratch_shapes=[pltpu.VMEM((rows, SUBLANES + c, 3 * D_GDN), F32),
                        pltpu.VMEM((rows, H_GDN // 2, PAIR, PAIR), F32),
                        pltpu.VMEM((rows * c, D_GDN), F32)],
        compiler_params=_seq_params(),
        name="gated_deltanet",
    )(p_gdn, misc, prm["conv_w"], prm["a_log"], prm["dt_bias"], prm["norm_w"], consts["tril"], consts["blk_gdn"],
      consts["exp_g"], consts["exp_b"], consts["exp_l"])


def _block_ones(n):
    i = jnp.arange(n) // HEAD_DIM
    return (i[:, None] == i[None, :]).astype(BF16)


def _constants(rows):
    c = CHUNK
    idx = jnp.arange(rows * c)
    same_chunk = (idx[:, None] // c) == (idx[None, :] // c)
    tile_k = lambda m, parts: jnp.tile(m, (parts, 1))
    consts = {"tril": jnp.tile((same_chunk & (idx[:, None] >= idx[None, :])).astype(BF16), (1, SUM_PARTS)),
              "blk_rwkv": tile_k(_block_ones(D_RWKV), NORM_PARTS), "blk_ret": tile_k(_block_ones(D_RET), NORM_PARTS),
              "blk_gdn": tile_k(_block_ones(D_GDN), NORM_PARTS), "ret_mask": _block_ones(D_RET)}
    rc = RET_CHUNK
    ridx = jnp.arange(rc)
    log_gamma = jnp.log(1.0 - 2.0 ** (-5.0 - jnp.arange(H_RET, dtype=F32)))
    rel = ridx[:, None] - ridx[None, :]
    consts["ret_decay"] = jnp.where(rel >= 0, jnp.exp(log_gamma[:, None, None] * jnp.maximum(rel, 0)), 0.0)
    lanes_gamma = jnp.repeat(log_gamma, HEAD_DIM)[None, :]
    consts["ret_qw"] = jnp.exp(lanes_gamma * (ridx[:, None] + 1))
    consts["ret_kw"] = jnp.exp(lanes_gamma * (rc - 1 - ridx[:, None]))
    consts["ret_cd"] = jnp.exp(lanes_gamma * rc)
    rows = jnp.arange(MISC_COLS)[:, None]
    wide_head = jnp.arange(H_GDN * LANES)[None, :] // LANES
    lane_head = jnp.arange(D_GDN)[None, :] // HEAD_DIM
    consts["exp_g"] = tile_k((rows == MISC_ALOG + wide_head).astype(BF16), SUM_PARTS)
    consts["exp_b"] = tile_k((rows == MISC_BETA + wide_head).astype(BF16), NORM_PARTS)
    consts["exp_l"] = tile_k((rows == MISC_ALOG + lane_head).astype(BF16), SUM_PARTS)
    return consts


def _pad_rows(w, top, total):
    return jnp.pad(w, ((top, total - top - w.shape[0]), (0, 0)))


def _pad_lanes(v, left, total):
    return jnp.pad(v, (left, total - left - v.shape[0]))[None, :]


def kernel(x, positions, norm_g, ffn_w_gate, ffn_w_up, ffn_w_down, w_in_first, w_in_rest, w_out, rwkv_mu, rwkv_w0, rwkv_w_up, rwkv_a0, rwkv_a_up, rwkv_g_up, rwkv_k_k, rwkv_k_a, rwkv_r_k, rwkv_ln_w, rwkv_ln_b, rwkv_mu_vres, rwkv_v0, rwkv_v_up, ret_gn_w, gdn_conv_w, gdn_A_log, gdn_dt_bias, gdn_norm_w):
    b, t, d = x.shape
    depth = norm_g.shape[0]
    n = b * t
    assert d == D_MODEL and n % ROW_TILE == 0 and t % RET_CHUNK == 0 and t % CHUNK == 0
    consts = _constants(_seq_rows(b))
    cos, sin = _rope_tables(positions)
    x = x.reshape(n, d)
    v_first = None
    wg, wu, wd, wo = (w.astype(BF16) for w in (ffn_w_gate, ffn_w_up, ffn_w_down, w_out))
    for l in range(depth):
        ng = norm_g[l][:, None, :]
        x = _ffn(x, ng[0], ng[1], wg, wu, wd, l, 0)

        w_in = w_in_first if l == 0 else w_in_rest[l - 1]
        gdn_small = w_in[:, RWKV_COLS + RET_COLS + GDN_MAIN_COLS:IN_COLS_FIRST]
        vres_cols = w_in[:, IN_COLS_FIRST:] if l > 0 else jnp.zeros((d, RWKV_VRES_RANK), F32)
        misc_w = jnp.concatenate([gdn_small, vres_cols], axis=1)
        misc_w = jnp.pad(misc_w, ((0, 0), (0, MISC_COLS - misc_w.shape[1])))
        w_packed = jnp.concatenate([w_in[:, :RWKV_COLS + RET_COLS + GDN_MAIN_COLS], misc_w], axis=1).astype(BF16)
        p_rwkv, p_ret, p_gdn, p_misc = _inproj(x, ng[2], w_packed)
        p_rwkv = p_rwkv.reshape(b, t, RWKV_COLS)
        p_ret = p_ret.reshape(b, t, RET_COLS)
        p_gdn = p_gdn.reshape(b, t, GDN_MAIN_COLS)
        p_misc = p_misc.reshape(b, t, MISC_COLS)

        rw = {"mu": rwkv_mu[l][None, :], "w0": rwkv_w0[l][None, :],
              "w_up": _pad_rows(rwkv_w_up[l], 0, LANES).astype(BF16), "a0": rwkv_a0[l][None, :],
              "a_up": _pad_rows(rwkv_a_up[l], RWKV_DECAY_RANK, LANES).astype(BF16),
              "g_up": rwkv_g_up[l].astype(BF16), "k_k": rwkv_k_k[l][None, :], "k_a": rwkv_k_a[l][None, :],
              "r_k": rwkv_r_k[l][None, :], "ln_w": rwkv_ln_w[l][None, :], "ln_b": rwkv_ln_b[l][None, :]}
        if l > 0:
            rw["mu_v"] = _pad_lanes(rwkv_mu_vres[l - 1], MISC_VRES, MISC_COLS)
            rw["v0"] = rwkv_v0[l - 1][None, :]
            rw["v_up"] = _pad_rows(rwkv_v_up[l - 1], MISC_VRES, MISC_COLS).astype(BF16)
        y_a, v_first = _rwkv(p_rwkv, p_misc, v_first, rw, consts)
        y_b = _ret(p_ret, cos, sin, ret_gn_w[l][None, :], consts)
        gd = {"conv_w": gdn_conv_w[l], "a_log": _pad_lanes(gdn_A_log[l], MISC_ALOG, MISC_COLS),
              "dt_bias": _pad_lanes(gdn_dt_bias[l], MISC_ALOG, MISC_COLS),
              "norm_w": jnp.tile(gdn_norm_w[l], H_GDN)[None, :]}
        y_c = _gdn(p_gdn, p_misc, gd, consts)

        x = _outproj(x, y_a.reshape(n, D_RWKV), y_b.reshape(n, D_RET), y_c.reshape(n, D_GDN), wo, ng[3], l)
        x = _ffn(x, ng[4], ng[5], wg, wu, wd, l, 1)
    return x.reshape(b, t, d)
```

```python
import functools
import math

import jax
import jax.numpy as jnp
from jax import lax
from jax.experimental import pallas as pl
from jax.experimental.pallas import tpu as pltpu

F32 = jnp.float32
BF16 = jnp.bfloat16

D_MODEL = 1024
D_FF = 2816
HEAD_DIM = 64
H_RWKV, H_RET, H_GDN = 6, 4, 6
D_RWKV, D_RET, D_GDN = H_RWKV * HEAD_DIM, H_RET * HEAD_DIM, H_GDN * HEAD_DIM
RWKV_DECAY_RANK, RWKV_ICLR_RANK, RWKV_VRES_RANK, RWKV_GATE_RANK = 64, 64, 32, 128
RWKV_COLS = 3 * D_RWKV + RWKV_DECAY_RANK + RWKV_ICLR_RANK + RWKV_GATE_RANK
RET_COLS = 4 * D_RET
GDN_MAIN_COLS = 4 * D_GDN
GDN_COLS = GDN_MAIN_COLS + 2 * H_GDN
IN_COLS_FIRST = RWKV_COLS + RET_COLS + GDN_COLS
GDN_CONV = 4
NORM_EPS = 1e-6
L2_EPS = 1e-6
RWKV_GN_EPS = HEAD_DIM * 1e-5
RET_GN_EPS = 1e-5
ROPE_THETA = 10000.0

LANES = 128
SUBLANES = 8
MISC_COLS = LANES
MISC_BETA, MISC_ALOG, MISC_VRES = 0, H_GDN, 2 * H_GDN
CHUNK = 64
RET_CHUNK = 128
PAIR = 2 * HEAD_DIM
SEQ_ROWS = 4
ROW_TILE = 512
FF_CHUNK = 512
SUM_PARTS = 3
NORM_PARTS = 1
VMEM_LIMIT = 56 * 1024 * 1024


def _dot(a, b):
    return jnp.dot(a.astype(BF16), b.astype(BF16), preferred_element_type=F32)


def _dot_nt(a, b):
    return lax.dot_general(a.astype(BF16), b.astype(BF16), (((1,), (1,)), ((), ())),
                           preferred_element_type=F32)


def _dot_tn(a, b):
    return lax.dot_general(a.astype(BF16), b.astype(BF16), (((0,), (0,)), ((), ())),
                           preferred_element_type=F32)


def _split(x, parts):
    out = []
    for _ in range(parts - 1):
        h = x.astype(BF16)
        out.append(h)
        x = x - h.astype(F32)
    out.append(x.astype(BF16))
    return out


def _dot_exact_rhs(a_tiled, x):
    parts = a_tiled.shape[1] // x.shape[0]
    return jnp.dot(a_tiled, jnp.concatenate(_split(x, parts), axis=0), preferred_element_type=F32)


def _dot_exact_lhs(x, b_tiled):
    parts = b_tiled.shape[0] // x.shape[1]
    return jnp.dot(jnp.concatenate(_split(x, parts), axis=1), b_tiled, preferred_element_type=F32)


def _rms(x, g):
    return x * lax.rsqrt(jnp.mean(x * x, axis=-1, keepdims=True) + NORM_EPS) * g


def _sigmoid(x):
    return 1.0 / (1.0 + jnp.exp(-x))


def _silu(x):
    return x * _sigmoid(x)


def _softplus(x):
    return jnp.maximum(x, 0.0) + jnp.log(1.0 + jnp.exp(-jnp.abs(x)))


def _stack_heads(x):
    lane = lax.broadcasted_iota(jnp.int32, x.shape, 1) % PAIR
    return jnp.concatenate([jnp.where(lane < HEAD_DIM, x, 0.0), jnp.where(lane >= HEAD_DIM, x, 0.0)], axis=0)


def _time_masks(c):
    i = lax.broadcasted_iota(jnp.int32, (c, 2 * c), 0)
    j = lax.broadcasted_iota(jnp.int32, (c, 2 * c), 1) % c
    return i > j, i >= j, i == j


assert CHUNK == HEAD_DIM
NEUMANN_LEVELS = int(math.log2(CHUNK))


def _power_step(p, x, last):
    n = p.shape[1]
    if last:
        return None, x + jnp.dot(p, _stack_heads(x.astype(BF16)), preferred_element_type=F32)
    z = jnp.dot(p, _stack_heads(jnp.concatenate([p, x.astype(BF16)], axis=1)), preferred_element_type=F32)
    return z[:, :n].astype(BF16), x + z[:, n:]


def _ffn_body(x_ref, gpre_ref, gpost_ref, wg_ref, wu_ref, wd_ref, o_ref, h_ref, a_ref):
    x = x_ref[...]
    h_ref[...] = _rms(x, gpre_ref[...]).astype(BF16)
    for c0 in range(0, D_FF, FF_CHUNK):
        w = min(FF_CHUNK, D_FF - c0)
        h = h_ref[...]
        g = jnp.dot(h, wg_ref[:, c0:c0 + w], preferred_element_type=F32)
        u = jnp.dot(h, wu_ref[:, c0:c0 + w], preferred_element_type=F32)
        a_ref[:, c0:c0 + w] = (_silu(g) * u).astype(BF16)
    y = jnp.dot(a_ref[...], wd_ref[...], preferred_element_type=F32)
    o_ref[...] = x + 0.5 * _rms(y, gpost_ref[...])


def _row_spec(width):
    return pl.BlockSpec((ROW_TILE, width), lambda i: (i, 0))


def _const_spec(shape):
    zeros = (0,) * len(shape)
    return pl.BlockSpec(shape, lambda *_: zeros, pipeline_mode=pl.Buffered(1))


def _dense_params():
    return pltpu.CompilerParams(dimension_semantics=("parallel",), vmem_limit_bytes=VMEM_LIMIT)


def _layer_spec(shape, *lead):
    return pl.BlockSpec((None,) * len(lead) + shape, lambda *_: lead + (0,) * len(shape),
                        pipeline_mode=pl.Buffered(1))


def _ffn(x, g_pre, g_post, wg, wu, wd, layer, half):
    n = x.shape[0]
    return pl.pallas_call(
        _ffn_body,
        grid=(n // ROW_TILE,),
        in_specs=[_row_spec(D_MODEL), _const_spec((1, D_MODEL)), _const_spec((1, D_MODEL)),
                  _layer_spec((D_MODEL, D_FF), layer, half), _layer_spec((D_MODEL, D_FF), layer, half),
                  _layer_spec((D_FF, D_MODEL), layer, half)],
        out_specs=_row_spec(D_MODEL),
        out_shape=jax.ShapeDtypeStruct((n, D_MODEL), F32),
        scratch_shapes=[pltpu.VMEM((ROW_TILE, D_MODEL), BF16), pltpu.VMEM((ROW_TILE, D_FF), BF16)],
        compiler_params=_dense_params(),
        name="ffn",
    )(x, g_pre, g_post, wg, wu, wd)


_IN_WIDTHS = (RWKV_COLS, RET_COLS, GDN_MAIN_COLS, MISC_COLS)
IN_COLS_PACKED = sum(_IN_WIDTHS)


def _inproj_body(x_ref, g_ref, w_ref, o_rwkv, o_ret, o_gdn, o_misc, h_ref):
    h_ref[...] = _rms(x_ref[...], g_ref[...]).astype(BF16)
    off = 0
    for o_ref in (o_rwkv, o_ret, o_gdn, o_misc):
        width = o_ref.shape[-1]
        for c0 in range(0, width, FF_CHUNK):
            w = min(FF_CHUNK, width - c0)
            o_ref[:, c0:c0 + w] = jnp.dot(h_ref[...], w_ref[:, off + c0:off + c0 + w],
                                          preferred_element_type=F32)
        off += width


def _inproj(x, g, w):
    n = x.shape[0]
    return pl.pallas_call(
        _inproj_body,
        grid=(n // ROW_TILE,),
        in_specs=[_row_spec(D_MODEL), _const_spec((1, D_MODEL)), _const_spec((D_MODEL, IN_COLS_PACKED))],
        out_specs=[_row_spec(w_) for w_ in _IN_WIDTHS],
        out_shape=[jax.ShapeDtypeStruct((n, w_), F32) for w_ in _IN_WIDTHS],
        scratch_shapes=[pltpu.VMEM((ROW_TILE, D_MODEL), BF16)],
        compiler_params=_dense_params(),
        name="inproj",
    )(x, g, w)


def _outproj_body(x_ref, ya_ref, yb_ref, yc_ref, w_ref, g_ref, o_ref):
    y_all = jnp.concatenate([ya_ref[...].astype(BF16), yb_ref[...].astype(BF16), yc_ref[...].astype(BF16)], axis=1)
    y = jnp.dot(y_all, w_ref[...], preferred_element_type=F32)
    o_ref[...] = x_ref[...] + _rms(y, g_ref[...])


def _outproj(x, ya, yb, yc, w, g, layer):
    n = x.shape[0]
    return pl.pallas_call(
        _outproj_body,
        grid=(n // ROW_TILE,),
        in_specs=[_row_spec(D_MODEL), _row_spec(D_RWKV), _row_spec(D_RET), _row_spec(D_GDN),
                  _layer_spec((D_RWKV + D_RET + D_GDN, D_MODEL), layer), _const_spec((1, D_MODEL))],
        out_specs=_row_spec(D_MODEL),
        out_shape=jax.ShapeDtypeStruct((n, D_MODEL), F32),
        compiler_params=_dense_params(),
        name="outproj",
    )(x, ya, yb, yc, w, g)


def _rope_body(pos_ref, invf_ref, sign_ref, cos_ref, sin_ref):
    ang = pos_ref[0] * invf_ref[...]
    cos_ref[0] = jnp.cos(ang)
    sin_ref[0] = jnp.sin(ang) * sign_ref[...]


def _rope_tables(positions):
    b, t = positions.shape
    half = HEAD_DIM // 2
    inv_freq = ROPE_THETA ** (-jnp.arange(0, HEAD_DIM, 2, dtype=F32) / HEAD_DIM)
    invf = jnp.tile(inv_freq, LANES // half)[None, :]
    sign = jnp.tile(jnp.concatenate([-jnp.ones((half,), F32), jnp.ones((half,), F32)]), LANES // HEAD_DIM)[None, :]
    pos = positions.astype(F32)[..., None]
    tt = min(t, 512)
    return pl.pallas_call(
        _rope_body,
        grid=(b, t // tt),
        in_specs=[pl.BlockSpec((1, tt, 1), lambda i, j: (i, j, 0)),
                  pl.BlockSpec((1, LANES), lambda i, j: (0, 0)),
                  pl.BlockSpec((1, LANES), lambda i, j: (0, 0))],
        out_specs=[pl.BlockSpec((1, tt, LANES), lambda i, j: (i, j, 0))] * 2,
        out_shape=[jax.ShapeDtypeStruct((b, t, LANES), F32)] * 2,
        compiler_params=pltpu.CompilerParams(dimension_semantics=("parallel", "parallel")),
        name="rope_tables",
    )(pos, invf, sign)


def _seq_spec(c, width, rows=1):
    return pl.BlockSpec((rows, c, width), lambda b, t: (b, t, 0))


def _seq_const(shape):
    zeros = (0,) * len(shape)
    return pl.BlockSpec(shape, lambda b, t: zeros)


def _seq_rows(b):
    return math.gcd(b, SEQ_ROWS)


def _seq_params():
    return pltpu.CompilerParams(dimension_semantics=("parallel", "arbitrary"), vmem_limit_bytes=VMEM_LIMIT)


def _shifted(buf_ref, x_ref, shifts):
    rows, c, _ = x_ref.shape
    width = buf_ref.shape[-1]
    out = [[] for _ in shifts]
    for r in range(rows):
        x = x_ref[r, :, 0:width]
        buf_ref[r, SUBLANES:SUBLANES + c, :] = x
        for i, s in enumerate(shifts):
            out[i].append(buf_ref[r, SUBLANES - s:SUBLANES - s + c, :])
        buf_ref[r, 0:SUBLANES, :] = x[c - SUBLANES:c, :]
    return [jnp.concatenate(o, axis=0) for o in out]


def _zero_tails(buf_ref):
    rows, _, width = buf_ref.shape
    buf_ref[:, 0:SUBLANES, :] = jnp.zeros((rows, SUBLANES, width), F32)


def _last_row_of_each_chunk(x, rows, c):
    return jnp.concatenate([jnp.broadcast_to(x[(r + 1) * c - 1:(r + 1) * c, :], (c, x.shape[1]))
                            for r in range(rows)], axis=0)


def _rwkv_body(has_vres, *refs):
    if has_vres:
        (p_ref, misc_ref, vf_ref, mu_ref, w0_ref, wup_ref, a0_ref, aup_ref, gup_ref, kk_ref, ka_ref, rk_ref,
         lnw_ref, lnb_ref, muv_ref, v0_ref, vup_ref, tril_ref, blk_ref,
         y_ref, buf_ref, bufv_ref, s_ref, ys_ref) = refs
    else:
        (p_ref, mu_ref, w0_ref, wup_ref, a0_ref, aup_ref, gup_ref, kk_ref, ka_ref, rk_ref,
         lnw_ref, lnb_ref, tril_ref, blk_ref,
         y_ref, vf_out_ref, buf_ref, s_ref, ys_ref) = refs
        bufv_ref = None
    rows, c, _ = p_ref.shape

    @pl.when(pl.program_id(1) == 0)
    def _():
        s_ref[...] = jnp.zeros(s_ref.shape, F32)
        _zero_tails(buf_ref)
        if bufv_ref is not None:
            _zero_tails(bufv_ref)

    p = jnp.concatenate([p_ref[r] for r in range(rows)], axis=0)
    (prev,) = _shifted(buf_ref, p_ref, (1,))
    ps = p + (prev - p) * mu_ref[...]
    rec = ps[:, 0:D_RWKV]
    k = ps[:, D_RWKV:2 * D_RWKV]
    v = ps[:, 2 * D_RWKV:3 * D_RWKV]
    wa = ps[:, 3 * D_RWKV:3 * D_RWKV + LANES]
    gc = ps[:, 3 * D_RWKV + LANES:RWKV_COLS]
    logw = -math.exp(-0.5) * _sigmoid(w0_ref[...] + _dot(jnp.tanh(wa), wup_ref[...]))
    a = _sigmoid(a0_ref[...] + _dot(wa, aup_ref[...]))
    g = _dot(_sigmoid(gc), gup_ref[...])
    if has_vres:
        pm = jnp.concatenate([misc_ref[r] for r in range(rows)], axis=0)
        (prevm,) = _shifted(bufv_ref, misc_ref, (1,))
        pmv = pm + (prevm - pm) * muv_ref[...]
        vf = jnp.concatenate([vf_ref[r] for r in range(rows)], axis=0)
        v = v + (vf - v) * _sigmoid(v0_ref[...] + _dot(pmv, vup_ref[...]))
    else:
        for r in range(rows):
            vf_out_ref[r] = v[r * c:(r + 1) * c, :]

    blk = blk_ref[...]
    kkr = k * kk_ref[...]
    kk = kkr * lax.rsqrt(_dot_exact_lhs(kkr * kkr, blk) + L2_EPS)
    k2 = k * (1.0 + (a - 1.0) * ka_ref[...])

    cw = _dot_exact_rhs(tril_ref[...], logw)
    cw_last = _last_row_of_each_chunk(cw, rows, c)
    dec_in = jnp.exp(cw)
    dec_ex = jnp.exp(cw - logw)
    dec_inv = jnp.exp(-cw)
    dec_tail = jnp.exp(cw_last - cw)
    dec_all = jnp.exp(cw_last)
    kka = kk * a
    b_p = (kk * dec_ex).astype(BF16)
    r_p = (rec * dec_in).astype(BF16)
    a_i = (-kka * dec_inv).astype(BF16)
    k_i = (k2 * dec_inv).astype(BF16)
    a_d = (-kka * dec_tail).astype(BF16)
    k_d = (k2 * dec_tail).astype(BF16)
    v_b = v.astype(BF16)

    strict, tril, _ = _time_masks(c)
    tril2 = jnp.concatenate([tril, tril], axis=1)
    probs = [(n, h, slice(n * c, (n + 1) * c), slice(h * PAIR, (h + 1) * PAIR))
             for n in range(rows) for h in range(H_RWKV // 2)]
    left = [jnp.concatenate([b_p[rs, sl], r_p[rs, sl]], axis=0) for _, _, rs, sl in probs]
    v_st = [_stack_heads(v_b[rs, sl]) for _, _, rs, sl in probs]
    m = [_dot_nt(lf, jnp.concatenate([_stack_heads(a_i[rs, sl]), _stack_heads(k_i[rs, sl])], axis=0))
         for lf, (_, _, rs, sl) in zip(left, probs)]
    x0 = [_dot_nt(lf, s_ref[n, h]) for lf, (n, h, _, _) in zip(left, probs)]
    u = [xi[0:c] + _dot(jnp.where(strict, mi[0:c, 2 * c:4 * c], 0.0), vi) for xi, mi, vi in zip(x0, m, v_st)]
    pw = [jnp.where(strict, mi[0:c, 0:2 * c], 0.0).astype(BF16) for mi in m]
    for j in range(NEUMANN_LEVELS):
        stepped = [_power_step(pi, ui, j == NEUMANN_LEVELS - 1) for pi, ui in zip(pw, u)]
        pw = [sp for sp, _ in stepped]
        u = [su for _, su in stepped]
    uv = [jnp.concatenate([_stack_heads(ui.astype(BF16)), vi], axis=0) for ui, vi in zip(u, v_st)]
    for i, (n, h, rs, sl) in enumerate(probs):
        ys_ref[rs, sl] = x0[i][c:2 * c] + _dot(jnp.where(tril2, m[i][c:2 * c, :], 0.0), uv[i])
    for i, (n, h, rs, sl) in enumerate(probs):
        tails = jnp.concatenate([_stack_heads(a_d[rs, sl]), _stack_heads(k_d[rs, sl])], axis=0)
        s_ref[n, h] = s_ref[n, h] * dec_all[(n + 1) * c - 1:(n + 1) * c, sl] + _dot_tn(uv[i], tails)
    y = ys_ref[...]

    inv_d = 1.0 / HEAD_DIM
    mean = _dot_exact_lhs(y, blk) * inv_d
    yc = y - mean
    var = _dot_exact_lhs(yc * yc, blk) * inv_d
    yn = yc * lax.rsqrt(var + RWKV_GN_EPS) * lnw_ref[...] + lnb_ref[...]
    bonus = _dot_exact_lhs(rec * k2 * rk_ref[...], blk) * v
    out = (yn + bonus) * g
    for r in range(rows):
        y_ref[r] = out[r * c:(r + 1) * c, :]


def _rwkv(p_rwkv, misc, v_first, prm, consts):
    b, t, _ = p_rwkv.shape
    c = CHUNK
    has_vres = v_first is not None
    row = lambda n: _seq_const((1, n))
    common = [prm["mu"], prm["w0"], prm["w_up"], prm["a0"], prm["a_up"], prm["g_up"], prm["k_k"], prm["k_a"],
              prm["r_k"], prm["ln_w"], prm["ln_b"]]
    common_specs = [row(RWKV_COLS), row(D_RWKV), _seq_const((LANES, D_RWKV)), row(D_RWKV),
                    _seq_const((LANES, D_RWKV)), _seq_const((RWKV_GATE_RANK, D_RWKV)), row(D_RWKV), row(D_RWKV),
                    row(D_RWKV), row(D_RWKV), row(D_RWKV)]
    rows = _seq_rows(b)
    const_in = [consts["tril"], consts["blk_rwkv"]]
    const_specs = [_seq_const(a.shape) for a in const_in]
    state = pltpu.VMEM((rows, H_RWKV // 2, PAIR, PAIR), F32)
    ys = pltpu.VMEM((rows * c, D_RWKV), F32)
    if has_vres:
        ins = [p_rwkv, misc, v_first] + common + [prm["mu_v"], prm["v0"], prm["v_up"]] + const_in
        in_specs = ([_seq_spec(c, RWKV_COLS, rows), _seq_spec(c, MISC_COLS, rows), _seq_spec(c, D_RWKV, rows)]
                    + common_specs + [row(MISC_COLS), row(D_RWKV), _seq_const((MISC_COLS, D_RWKV))] + const_specs)
        out_specs = _seq_spec(c, D_RWKV, rows)
        out_shape = jax.ShapeDtypeStruct((b, t, D_RWKV), F32)
        scratch = [pltpu.VMEM((rows, SUBLANES + c, RWKV_COLS), F32),
                   pltpu.VMEM((rows, SUBLANES + c, MISC_COLS), F32), state, ys]
    else:
        ins = [p_rwkv] + common + const_in
        in_specs = [_seq_spec(c, RWKV_COLS, rows)] + common_specs + const_specs
        out_specs = [_seq_spec(c, D_RWKV, rows)] * 2
        out_shape = [jax.ShapeDtypeStruct((b, t, D_RWKV), F32)] * 2
        scratch = [pltpu.VMEM((rows, SUBLANES + c, RWKV_COLS), F32), state, ys]
    out = pl.pallas_call(
        functools.partial(_rwkv_body, has_vres),
        grid=(b // rows, t // c),
        in_specs=in_specs, out_specs=out_specs, out_shape=out_shape, scratch_shapes=scratch,
        compiler_params=_seq_params(),
        name="rwkv7_vres" if has_vres else "rwkv7_first",
    )(*ins)
    return (out, v_first) if has_vres else (out[0], out[1])


def _ret_body(p_ref, cos_ref, sin_ref, gnw_ref, dec_ref, qw_ref, kw_ref, cd_ref, blk_ref, mask_ref, y_ref, r_ref):
    rows, c, _ = p_ref.shape

    @pl.when(pl.program_id(1) == 0)
    def _():
        r_ref[...] = jnp.zeros(r_ref.shape, F32)

    stack = lambda ref, reps: jnp.concatenate(
        [jnp.concatenate([ref[n]] * reps, axis=1) for n in range(rows)], axis=0)
    p = stack(p_ref, 1)
    q = p[:, 0:D_RET]
    k = p[:, D_RET:2 * D_RET]
    v = p[:, 2 * D_RET:3 * D_RET].astype(BF16)
    gate = p[:, 3 * D_RET:4 * D_RET]
    cos = stack(cos_ref, D_RET // LANES)
    sin = stack(sin_ref, D_RET // LANES)
    lane = lax.broadcasted_iota(jnp.int32, q.shape, 1)
    first_half = (lane % HEAD_DIM) < (HEAD_DIM // 2)
    half = HEAD_DIM // 2

    def rope(x):
        partner = jnp.where(first_half, pltpu.roll(x, D_RET - half, 1), pltpu.roll(x, half, 1))
        return x * cos + partner * sin

    q = rope(q) * HEAD_DIM ** -0.5
    k = rope(k)
    q_in = (q * jnp.concatenate([qw_ref[...]] * rows, axis=0)).astype(BF16)
    k_out = (k * jnp.concatenate([kw_ref[...]] * rows, axis=0)).astype(BF16)
    q = q.astype(BF16)
    k = k.astype(BF16)
    chunk_lane = lax.broadcasted_iota(jnp.int32, (c, D_RET), 1)
    in_head = [(chunk_lane // HEAD_DIM) == h for h in range(H_RET)]
    chunks = [(n, slice(n * c, (n + 1) * c)) for n in range(rows)]
    scores = [[(_dot_nt(jnp.where(m, q[rs], 0.0), k[rs]) * dec_ref[h]).astype(BF16) for h, m in enumerate(in_head)]
              for _, rs in chunks]
    outs = []
    for (n, rs), sc in zip(chunks, scores):
        lhs = jnp.concatenate(sc + [q_in[rs]], axis=1)
        rhs = jnp.concatenate([jnp.where(m, v[rs], 0.0) for m in in_head] + [r_ref[n].astype(BF16)], axis=0)
        outs.append(jnp.dot(lhs, rhs, preferred_element_type=F32))
    for n, rs in chunks:
        r_ref[n] = r_ref[n] * cd_ref[...] + _dot_tn(k_out[rs], v[rs]) * mask_ref[...].astype(F32)
    o = jnp.concatenate(outs, axis=0)

    blk = blk_ref[...]
    inv_d = 1.0 / HEAD_DIM
    mean = _dot_exact_lhs(o, blk) * inv_d
    oc = o - mean
    var = _dot_exact_lhs(oc * oc, blk) * inv_d
    out = _silu(gate) * (oc * lax.rsqrt(var + RET_GN_EPS) * gnw_ref[...])
    for n, rs in chunks:
        y_ref[n] = out[rs]


def _ret(p_ret, cos, sin, gn_w, consts):
    b, t, _ = p_ret.shape
    c = RET_CHUNK
    rows = _seq_rows(b)
    tables = [consts[name] for name in ("ret_decay", "ret_qw", "ret_kw", "ret_cd", "blk_ret", "ret_mask")]
    return pl.pallas_call(
        _ret_body,
        grid=(b // rows, t // c),
        in_specs=[_seq_spec(c, RET_COLS, rows), _seq_spec(c, LANES, rows), _seq_spec(c, LANES, rows),
                  _seq_const((1, D_RET))] + [_seq_const(a.shape) for a in tables],
        out_specs=_seq_spec(c, D_RET, rows),
        out_shape=jax.ShapeDtypeStruct((b, t, D_RET), F32),
        scratch_shapes=[pltpu.VMEM((rows, D_RET, D_RET), F32)],
        compiler_params=_seq_params(),
        name="retention",
    )(p_ret, cos, sin, gn_w, *tables)


def _gdn_body(p_ref, misc_ref, convw_ref, alog_ref, dtb_ref, nw_ref, tril_ref, blk_ref, eg_ref, eb_ref, el_ref,
              y_ref, buf_ref, s_ref, os_ref):
    rows, c, _ = p_ref.shape

    @pl.when(pl.program_id(1) == 0)
    def _():
        s_ref[...] = jnp.zeros(s_ref.shape, F32)
        _zero_tails(buf_ref)

    gate = jnp.concatenate([p_ref[r, :, 3 * D_GDN:4 * D_GDN] for r in range(rows)], axis=0)
    taps = _shifted(buf_ref, p_ref, tuple(GDN_CONV - 1 - j for j in range(GDN_CONV)))
    conv = taps[0] * convw_ref[0:1, :]
    for j in range(1, GDN_CONV):
        conv = conv + taps[j] * convw_ref[j:j + 1, :]
    qkv = _silu(conv)
    blk = blk_ref[...]
    q = qkv[:, 0:D_GDN]
    k = qkv[:, D_GDN:2 * D_GDN]
    v = qkv[:, 2 * D_GDN:3 * D_GDN]
    q = q * lax.rsqrt(_dot_exact_lhs(q * q, blk) + L2_EPS) * HEAD_DIM ** -0.5
    k = k * lax.rsqrt(_dot_exact_lhs(k * k, blk) + L2_EPS)

    m = jnp.concatenate([misc_ref[r] for r in range(rows)], axis=0)
    beta = _sigmoid(m)
    g = -jnp.exp(alog_ref[...]) * _softplus(m + dtb_ref[...])
    gcs = _dot_exact_rhs(tril_ref[...], g)
    g_wide = _dot_exact_lhs(gcs, eg_ref[...])
    g_lane = _dot_exact_lhs(gcs, el_ref[...])
    b_lane = _dot_exact_lhs(beta, eb_ref[...])
    g_end = _last_row_of_each_chunk(g_lane, rows, c)
    eg = jnp.exp(g_lane)
    kb = k * b_lane
    left = jnp.concatenate([kb.astype(BF16), q.astype(BF16)], axis=1)
    rhs_uw = jnp.concatenate([(v * b_lane).astype(BF16), (kb * eg).astype(BF16)], axis=1)
    q_e = (q * eg).astype(BF16)
    k_tail = (k * jnp.exp(g_end - g_lane)).astype(BF16)
    s_decay = jnp.exp(g_end)
    k_b16 = k.astype(BF16)

    strict, tril, diag = _time_masks(c)
    eye = jnp.where(diag, 1.0, 0.0)
    probs = [(n, h, slice(n * c, (n + 1) * c), slice(h * PAIR, (h + 1) * PAIR))
             for n in range(rows) for h in range(H_GDN // 2)]
    qk, decay = [], []
    for n, h, rs, sl in probs:
        kq = jnp.concatenate([left[rs, sl], left[rs, D_GDN + h * PAIR:D_GDN + (h + 1) * PAIR]], axis=0)
        qk.append(_dot_nt(kq, _stack_heads(k_b16[rs, sl])))
        w0, w1 = 2 * h * LANES, (2 * h + 1) * LANES
        g_cols = jnp.concatenate([g_wide[rs, w0:w0 + LANES], g_wide[rs, w1:w1 + LANES]], axis=0)
        g_rows = g_cols.T[0:c, :]
        decay.append(jnp.where(tril, jnp.exp(jnp.where(tril, g_lane[rs, sl] - g_rows, 0.0)), 0.0))
    lmat = [jnp.where(strict, qk_[0:c] * d_, 0.0) for qk_, d_ in zip(qk, decay)]
    tm = [eye - l_ for l_ in lmat]
    pw = [l_.astype(BF16) for l_ in lmat]
    pw = [jnp.dot(p_, _stack_heads(p_), preferred_element_type=F32).astype(BF16) for p_ in pw]
    for j in range(1, NEUMANN_LEVELS):
        stepped = [_power_step(p_, t_, j == NEUMANN_LEVELS - 1) for p_, t_ in zip(pw, tm)]
        pw = [sp for sp, _ in stepped]
        tm = [st for _, st in stepped]
    uw = [_dot(t_, _stack_heads(jnp.concatenate([rhs_uw[rs, sl], rhs_uw[rs, D_GDN + h * PAIR:D_GDN + (h + 1) * PAIR]],
                                                axis=1)))
          for t_, (_, h, rs, sl) in zip(tm, probs)]
    v_new = [_stack_heads((uw_[:, 0:LANES] - _dot(uw_[:, LANES:2 * LANES], s_ref[n, h])).astype(BF16))
             for uw_, (n, h, _, _) in zip(uw, probs)]
    for i, (n, h, rs, sl) in enumerate(probs):
        attn = (qk[i][c:2 * c] * decay[i]).astype(BF16)
        os_ref[rs, sl] = jnp.dot(jnp.concatenate([q_e[rs, sl], attn], axis=1),
                                 jnp.concatenate([s_ref[n, h].astype(BF16), v_new[i]], axis=0),
                                 preferred_element_type=F32)
    for i, (n, h, rs, sl) in enumerate(probs):
        s_ref[n, h] = (s_ref[n, h] * s_decay[(n + 1) * c - 1:(n + 1) * c, sl]
                       + _dot_tn(_stack_heads(k_tail[rs, sl]), v_new[i]))
    o = os_ref[...]
    ms = _dot_exact_lhs(o * o, blk) * (1.0 / HEAD_DIM)
    out = o * lax.rsqrt(ms + NORM_EPS) * nw_ref[...] * _silu(gate)
    for r in range(rows):
        y_ref[r] = out[r * c:(r + 1) * c, :]


def _gdn(p_gdn, misc, prm, consts):
    b, t, _ = p_gdn.shape
    c = CHUNK
    rows = _seq_rows(b)
    return pl.pallas_call(
        _gdn_body,
        grid=(b // rows, t // c),
        in_specs=[_seq_spec(c, GDN_MAIN_COLS, rows), _seq_spec(c, MISC_COLS, rows),
                  _seq_const((GDN_CONV, 3 * D_GDN)),
                  _seq_const((1, MISC_COLS)), _seq_const((1, MISC_COLS)), _seq_const((1, D_GDN)),
                  ] + [_seq_const(consts[name].shape) for name in ("tril", "blk_gdn", "exp_g", "exp_b", "exp_l")],
        out_specs=_seq_spec(c, D_GDN, rows),
        out_shape=jax.ShapeDtypeStruct((b, t, D_GDN), F32),
        scratch_shapes=[pltpu.VMEM((rows, SUBLANES + c, 3 * D_GDN), F32),
                        pltpu.VMEM((rows, H_GDN // 2, PAIR, PAIR), F32),
                        pltpu.VMEM((rows * c, D_GDN), F32)],
        compiler_params=_seq_params(),
        name="gated_deltanet",
    )(p_gdn, misc, prm["conv_w"], prm["a_log"], prm["dt_bias"], prm["norm_w"], consts["tril"], consts["blk_gdn"],
      consts["exp_g"], consts["exp_b"], consts["exp_l"])


def _block_ones(n):
    i = jnp.arange(n) // HEAD_DIM
    return (i[:, None] == i[None, :]).astype(BF16)


def _constants(rows):
    c = CHUNK
    idx = jnp.arange(rows * c)
    same_chunk = (idx[:, None] // c) == (idx[None, :] // c)
    tile_k = lambda m, parts: jnp.tile(m, (parts, 1))
    consts = {"tril": jnp.tile((same_chunk & (idx[:, None] >= idx[None, :])).astype(BF16), (1, SUM_PARTS)),
              "blk_rwkv": tile_k(_block_ones(D_RWKV), NORM_PARTS), "blk_ret": tile_k(_block_ones(D_RET), NORM_PARTS),
              "blk_gdn": tile_k(_block_ones(D_GDN), NORM_PARTS), "ret_mask": _block_ones(D_RET)}
    rc = RET_CHUNK
    ridx = jnp.arange(rc)
    log_gamma = jnp.log(1.0 - 2.0 ** (-5.0 - jnp.arange(H_RET, dtype=F32)))
    rel = ridx[:, None] - ridx[None, :]
    consts["ret_decay"] = jnp.where(rel >= 0, jnp.exp(log_gamma[:, None, None] * jnp.maximum(rel, 0)), 0.0)
    lanes_gamma = jnp.repeat(log_gamma, HEAD_DIM)[None, :]
    consts["ret_qw"] = jnp.exp(lanes_gamma * (ridx[:, None] + 1))
    consts["ret_kw"] = jnp.exp(lanes_gamma * (rc - 1 - ridx[:, None]))
    consts["ret_cd"] = jnp.exp(lanes_gamma * rc)
    rows = jnp.arange(MISC_COLS)[:, None]
    wide_head = jnp.arange(H_GDN * LANES)[None, :] // LANES
    lane_head = jnp.arange(D_GDN)[None, :] // HEAD_DIM
    consts["exp_g"] = tile_k((rows == MISC_ALOG + wide_head).astype(BF16), SUM_PARTS)
    consts["exp_b"] = tile_k((rows == MISC_BETA + lane_head).astype(BF16), NORM_PARTS)
    consts["exp_l"] = tile_k((rows == MISC_ALOG + lane_head).astype(BF16), SUM_PARTS)
    return consts


def _pad_rows(w, top, total):
    return jnp.pad(w, ((top, total - top - w.shape[0]), (0, 0)))


def _pad_lanes(v, left, total):
    return jnp.pad(v, (left, total - left - v.shape[0]))[None, :]


def kernel(x, positions, norm_g, ffn_w_gate, ffn_w_up, ffn_w_down, w_in_first, w_in_rest, w_out, rwkv_mu, rwkv_w0, rwkv_w_up, rwkv_a0, rwkv_a_up, rwkv_g_up, rwkv_k_k, rwkv_k_a, rwkv_r_k, rwkv_ln_w, rwkv_ln_b, rwkv_mu_vres, rwkv_v0, rwkv_v_up, ret_gn_w, gdn_conv_w, gdn_A_log, gdn_dt_bias, gdn_norm_w):
    b, t, d = x.shape
    depth = norm_g.shape[0]
    n = b * t
    assert d == D_MODEL and n % ROW_TILE == 0 and t % RET_CHUNK == 0 and t % CHUNK == 0
    consts = _constants(_seq_rows(b))
    cos, sin = _rope_tables(positions)
    x = x.reshape(n, d)
    v_first = None
    wg, wu, wd, wo = (w.astype(BF16) for w in (ffn_w_gate, ffn_w_up, ffn_w_down, w_out))
    for l in range(depth):
        ng = norm_g[l][:, None, :]
        x = _ffn(x, ng[0], ng[1], wg, wu, wd, l, 0)

        w_in = w_in_first if l == 0 else w_in_rest[l - 1]
        gdn_small = w_in[:, RWKV_COLS + RET_COLS + GDN_MAIN_COLS:IN_COLS_FIRST]
        vres_cols = w_in[:, IN_COLS_FIRST:] if l > 0 else jnp.zeros((d, RWKV_VRES_RANK), F32)
        misc_w = jnp.concatenate([gdn_small, vres_cols], axis=1)
        misc_w = jnp.pad(misc_w, ((0, 0), (0, MISC_COLS - misc_w.shape[1])))
        w_packed = jnp.concatenate([w_in[:, :RWKV_COLS + RET_COLS + GDN_MAIN_COLS], misc_w], axis=1).astype(BF16)
        p_rwkv, p_ret, p_gdn, p_misc = _inproj(x, ng[2], w_packed)
        p_rwkv = p_rwkv.reshape(b, t, RWKV_COLS)
        p_ret = p_ret.reshape(b, t, RET_COLS)
        p_gdn = p_gdn.reshape(b, t, GDN_MAIN_COLS)
        p_misc = p_misc.reshape(b, t, MISC_COLS)

        rw = {"mu": rwkv_mu[l][None, :], "w0": rwkv_w0[l][None, :],
              "w_up": _pad_rows(rwkv_w_up[l], 0, LANES).astype(BF16), "a0": rwkv_a0[l][None, :],
              "a_up": _pad_rows(rwkv_a_up[l], RWKV_DECAY_RANK, LANES).astype(BF16),
              "g_up": rwkv_g_up[l].astype(BF16), "k_k": rwkv_k_k[l][None, :], "k_a": rwkv_k_a[l][None, :],
              "r_k": rwkv_r_k[l][None, :], "ln_w": rwkv_ln_w[l][None, :], "ln_b": rwkv_ln_b[l][None, :]}
        if l > 0:
            rw["mu_v"] = _pad_lanes(rwkv_mu_vres[l - 1], MISC_VRES, MISC_COLS)
            rw["v0"] = rwkv_v0[l - 1][None, :]
            rw["v_up"] = _pad_rows(rwkv_v_up[l - 1], MISC_VRES, MISC_COLS).astype(BF16)
        y_a, v_first = _rwkv(p_rwkv, p_misc, v_first, rw, consts)
        y_b = _ret(p_ret, cos, sin, ret_gn_w[l][None, :], consts)
        gd = {"conv_w": gdn_conv_w[l], "a_log": _pad_lanes(gdn_A_log[l], MISC_ALOG, MISC_COLS),
              "dt_bias": _pad_lanes(gdn_dt_bias[l], MISC_ALOG, MISC_COLS),
              "norm_w": jnp.tile(gdn_norm_w[l], H_GDN)[None, :]}
        y_c = _gdn(p_gdn, p_misc, gd, consts)

        x = _outproj(x, y_a.reshape(n, D_RWKV), y_b.reshape(n, D_RET), y_c.reshape(n, D_GDN), wo, ng[3], l)
        x = _ffn(x, ng[4], ng[5], wg, wu, wd, l, 1)
    return x.reshape(b, t, d)
```

```python
import functools
import math

import jax
import jax.numpy as jnp
from jax import lax
from jax.experimental import pallas as pl
from jax.experimental.pallas import tpu as pltpu

F32 = jnp.float32
BF16 = jnp.bfloat16

D_MODEL = 1024
D_FF = 2816
HEAD_DIM = 64
H_RWKV, H_RET, H_GDN = 6, 4, 6
D_RWKV, D_RET, D_GDN = H_RWKV * HEAD_DIM, H_RET * HEAD_DIM, H_GDN * HEAD_DIM
RWKV_DECAY_RANK, RWKV_ICLR_RANK, RWKV_VRES_RANK, RWKV_GATE_RANK = 64, 64, 32, 128
RWKV_COLS = 3 * D_RWKV + RWKV_DECAY_RANK + RWKV_ICLR_RANK + RWKV_GATE_RANK
RET_COLS = 4 * D_RET
GDN_MAIN_COLS = 4 * D_GDN
GDN_COLS = GDN_MAIN_COLS + 2 * H_GDN
IN_COLS_FIRST = RWKV_COLS + RET_COLS + GDN_COLS
GDN_CONV = 4
NORM_EPS = 1e-6
L2_EPS = 1e-6
RWKV_GN_EPS = HEAD_DIM * 1e-5
RET_GN_EPS = 1e-5
ROPE_THETA = 10000.0

LANES = 128
SUBLANES = 8
MISC_COLS = LANES
MISC_BETA, MISC_ALOG, MISC_VRES = 0, H_GDN, 2 * H_GDN
CHUNK = 64
RET_CHUNK = 128
PAIR = 2 * HEAD_DIM
SEQ_ROWS = 8
ROW_TILE = 512
FF_CHUNK = 512
SUM_PARTS = 3
NORM_PARTS = 1
VMEM_LIMIT = 56 * 1024 * 1024


def _dot(a, b):
    return jnp.dot(a.astype(BF16), b.astype(BF16), preferred_element_type=F32)


def _dot_nt(a, b):
    return lax.dot_general(a.astype(BF16), b.astype(BF16), (((1,), (1,)), ((), ())),
                           preferred_element_type=F32)


def _dot_tn(a, b):
    return lax.dot_general(a.astype(BF16), b.astype(BF16), (((0,), (0,)), ((), ())),
                           preferred_element_type=F32)


def _split(x, parts):
    out = []
    for _ in range(parts - 1):
        h = x.astype(BF16)
        out.append(h)
        x = x - h.astype(F32)
    out.append(x.astype(BF16))
    return out


def _dot_exact_rhs(a_tiled, x):
    parts = a_tiled.shape[1] // x.shape[0]
    return jnp.dot(a_tiled, jnp.concatenate(_split(x, parts), axis=0), preferred_element_type=F32)


def _dot_exact_lhs(x, b_tiled):
    parts = b_tiled.shape[0] // x.shape[1]
    return jnp.dot(jnp.concatenate(_split(x, parts), axis=1), b_tiled, preferred_element_type=F32)


def _rms(x, g):
    return x * lax.rsqrt(jnp.mean(x * x, axis=-1, keepdims=True) + NORM_EPS) * g


def _sigmoid(x):
    return 1.0 / (1.0 + jnp.exp(-x))


def _silu(x):
    return x * _sigmoid(x)


def _softplus(x):
    return jnp.maximum(x, 0.0) + jnp.log(1.0 + jnp.exp(-jnp.abs(x)))


def _stack_heads(x):
    lane = lax.broadcasted_iota(jnp.int32, x.shape, 1) % PAIR
    return jnp.concatenate([jnp.where(lane < HEAD_DIM, x, 0.0), jnp.where(lane >= HEAD_DIM, x, 0.0)], axis=0)


def _time_masks(c):
    i = lax.broadcasted_iota(jnp.int32, (c, 2 * c), 0)
    j = lax.broadcasted_iota(jnp.int32, (c, 2 * c), 1) % c
    return i > j, i >= j, i == j


assert CHUNK == HEAD_DIM
NEUMANN_LEVELS = int(math.log2(CHUNK))


def _power_step(p, x, last):
    n = p.shape[1]
    if last:
        return None, x + jnp.dot(p, _stack_heads(x.astype(BF16)), preferred_element_type=F32)
    z = jnp.dot(p, _stack_heads(jnp.concatenate([p, x.astype(BF16)], axis=1)), preferred_element_type=F32)
    return z[:, :n].astype(BF16), x + z[:, n:]


def _ffn_body(after_mixer, *refs):
    if after_mixer:
        (x_ref, ya_ref, yb_ref, yc_ref, wo_ref, gmix_ref,
         gpre_ref, gpost_ref, wg_ref, wu_ref, wd_ref, o_ref, h_ref, a_ref) = refs
        y_all = jnp.concatenate([r[...].astype(BF16) for r in (ya_ref, yb_ref, yc_ref)], axis=1)
        x = x_ref[...] + _rms(jnp.dot(y_all, wo_ref[...], preferred_element_type=F32), gmix_ref[...])
    else:
        x_ref, gpre_ref, gpost_ref, wg_ref, wu_ref, wd_ref, o_ref, h_ref, a_ref = refs
        x = x_ref[...]
    h_ref[...] = _rms(x, gpre_ref[...]).astype(BF16)
    for c0 in range(0, D_FF, FF_CHUNK):
        w = min(FF_CHUNK, D_FF - c0)
        h = h_ref[...]
        g = jnp.dot(h, wg_ref[:, c0:c0 + w], preferred_element_type=F32)
        u = jnp.dot(h, wu_ref[:, c0:c0 + w], preferred_element_type=F32)
        a_ref[:, c0:c0 + w] = (_silu(g) * u).astype(BF16)
    y = jnp.dot(a_ref[...], wd_ref[...], preferred_element_type=F32)
    o_ref[...] = x + 0.5 * _rms(y, gpost_ref[...])


def _row_spec(width):
    return pl.BlockSpec((ROW_TILE, width), lambda i: (i, 0))


def _const_spec(shape):
    zeros = (0,) * len(shape)
    return pl.BlockSpec(shape, lambda *_: zeros, pipeline_mode=pl.Buffered(1))


def _dense_params():
    return pltpu.CompilerParams(dimension_semantics=("parallel",), vmem_limit_bytes=VMEM_LIMIT)


def _layer_spec(shape, *lead):
    return pl.BlockSpec((None,) * len(lead) + shape, lambda *_: lead + (0,) * len(shape),
                        pipeline_mode=pl.Buffered(1))


def _ffn(x, g_pre, g_post, wg, wu, wd, layer, half, mixer=None):
    n = x.shape[0]
    ins, in_specs = [x], [_row_spec(D_MODEL)]
    if mixer is not None:
        ya, yb, yc, wo, g_mix = mixer
        ins += [ya, yb, yc, wo, g_mix]
        in_specs += [_row_spec(D_RWKV), _row_spec(D_RET), _row_spec(D_GDN),
                     _layer_spec((D_RWKV + D_RET + D_GDN, D_MODEL), layer), _const_spec((1, D_MODEL))]
    ins += [g_pre, g_post, wg, wu, wd]
    in_specs += [_const_spec((1, D_MODEL)), _const_spec((1, D_MODEL)),
                 _layer_spec((D_MODEL, D_FF), layer, half), _layer_spec((D_MODEL, D_FF), layer, half),
                 _layer_spec((D_FF, D_MODEL), layer, half)]
    return pl.pallas_call(
        functools.partial(_ffn_body, mixer is not None),
        grid=(n // ROW_TILE,),
        in_specs=in_specs,
        out_specs=_row_spec(D_MODEL),
        out_shape=jax.ShapeDtypeStruct((n, D_MODEL), F32),
        scratch_shapes=[pltpu.VMEM((ROW_TILE, D_MODEL), BF16), pltpu.VMEM((ROW_TILE, D_FF), BF16)],
        compiler_params=_dense_params(),
        name="ffn_after_mixer" if mixer is not None else "ffn",
    )(*ins)


_IN_WIDTHS = (RWKV_COLS, RET_COLS, GDN_MAIN_COLS, MISC_COLS)
IN_COLS_PACKED = sum(_IN_WIDTHS)


def _inproj_body(x_ref, g_ref, w_ref, o_rwkv, o_ret, o_gdn, o_misc, h_ref):
    h_ref[...] = _rms(x_ref[...], g_ref[...]).astype(BF16)
    off = 0
    for o_ref in (o_rwkv, o_ret, o_gdn, o_misc):
        width = o_ref.shape[-1]
        for c0 in range(0, width, FF_CHUNK):
            w = min(FF_CHUNK, width - c0)
            o_ref[:, c0:c0 + w] = jnp.dot(h_ref[...], w_ref[:, off + c0:off + c0 + w],
                                          preferred_element_type=F32)
        off += width


def _inproj(x, g, w):
    n = x.shape[0]
    return pl.pallas_call(
        _inproj_body,
        grid=(n // ROW_TILE,),
        in_specs=[_row_spec(D_MODEL), _const_spec((1, D_MODEL)), _const_spec((D_MODEL, IN_COLS_PACKED))],
        out_specs=[_row_spec(w_) for w_ in _IN_WIDTHS],
        out_shape=[jax.ShapeDtypeStruct((n, w_), F32) for w_ in _IN_WIDTHS],
        scratch_shapes=[pltpu.VMEM((ROW_TILE, D_MODEL), BF16)],
        compiler_params=_dense_params(),
        name="inproj",
    )(x, g, w)


def _rope_body(pos_ref, invf_ref, sign_ref, cos_ref, sin_ref):
    ang = pos_ref[0] * invf_ref[...]
    cos_ref[0] = jnp.cos(ang)
    sin_ref[0] = jnp.sin(ang) * sign_ref[...]


def _rope_tables(positions):
    b, t = positions.shape
    half = HEAD_DIM // 2
    inv_freq = ROPE_THETA ** (-jnp.arange(0, HEAD_DIM, 2, dtype=F32) / HEAD_DIM)
    invf = jnp.tile(inv_freq, LANES // half)[None, :]
    sign = jnp.tile(jnp.concatenate([-jnp.ones((half,), F32), jnp.ones((half,), F32)]), LANES // HEAD_DIM)[None, :]
    pos = positions.astype(F32)[..., None]
    tt = min(t, 512)
    return pl.pallas_call(
        _rope_body,
        grid=(b, t // tt),
        in_specs=[pl.BlockSpec((1, tt, 1), lambda i, j: (i, j, 0)),
                  pl.BlockSpec((1, LANES), lambda i, j: (0, 0)),
                  pl.BlockSpec((1, LANES), lambda i, j: (0, 0))],
        out_specs=[pl.BlockSpec((1, tt, LANES), lambda i, j: (i, j, 0))] * 2,
        out_shape=[jax.ShapeDtypeStruct((b, t, LANES), F32)] * 2,
        compiler_params=pltpu.CompilerParams(dimension_semantics=("parallel", "parallel")),
        name="rope_tables",
    )(pos, invf, sign)


def _seq_spec(c, width, rows=1):
    return pl.BlockSpec((rows, c, width), lambda b, t: (b, t, 0))


def _seq_const(shape):
    zeros = (0,) * len(shape)
    return pl.BlockSpec(shape, lambda b, t: zeros)


def _seq_rows(b):
    return math.gcd(b, SEQ_ROWS)


def _seq_params():
    return pltpu.CompilerParams(dimension_semantics=("parallel", "arbitrary"), vmem_limit_bytes=VMEM_LIMIT)


def _shifted(buf_ref, x_ref, shifts):
    rows, c, _ = x_ref.shape
    width = buf_ref.shape[-1]
    out = [[] for _ in shifts]
    for r in range(rows):
        x = x_ref[r, :, 0:width]
        buf_ref[r, SUBLANES:SUBLANES + c, :] = x
        for i, s in enumerate(shifts):
            out[i].append(buf_ref[r, SUBLANES - s:SUBLANES - s + c, :])
        buf_ref[r, 0:SUBLANES, :] = x[c - SUBLANES:c, :]
    return [jnp.concatenate(o, axis=0) for o in out]


def _zero_tails(buf_ref):
    rows, _, width = buf_ref.shape
    buf_ref[:, 0:SUBLANES, :] = jnp.zeros((rows, SUBLANES, width), F32)


def _last_row_of_each_chunk(x, rows, c):
    return jnp.concatenate([jnp.broadcast_to(x[(r + 1) * c - 1:(r + 1) * c, :], (c, x.shape[1]))
                            for r in range(rows)], axis=0)


def _rwkv_body(has_vres, *refs):
    if has_vres:
        (p_ref, misc_ref, vf_ref, mu_ref, w0_ref, wup_ref, a0_ref, aup_ref, gup_ref, kk_ref, ka_ref, rk_ref,
         lnw_ref, lnb_ref, muv_ref, v0_ref, vup_ref, tril_ref, blk_ref,
         y_ref, buf_ref, bufv_ref, s_ref, ys_ref) = refs
    else:
        (p_ref, mu_ref, w0_ref, wup_ref, a0_ref, aup_ref, gup_ref, kk_ref, ka_ref, rk_ref,
         lnw_ref, lnb_ref, tril_ref, blk_ref,
         y_ref, vf_out_ref, buf_ref, s_ref, ys_ref) = refs
        bufv_ref = None
    rows, c, _ = p_ref.shape

    @pl.when(pl.program_id(1) == 0)
    def _():
        s_ref[...] = jnp.zeros(s_ref.shape, F32)
        _zero_tails(buf_ref)
        if bufv_ref is not None:
            _zero_tails(bufv_ref)

    p = jnp.concatenate([p_ref[r] for r in range(rows)], axis=0)
    (prev,) = _shifted(buf_ref, p_ref, (1,))
    ps = p + (prev - p) * mu_ref[...]
    rec = ps[:, 0:D_RWKV]
    k = ps[:, D_RWKV:2 * D_RWKV]
    v = ps[:, 2 * D_RWKV:3 * D_RWKV]
    wa = ps[:, 3 * D_RWKV:3 * D_RWKV + LANES]
    gc = ps[:, 3 * D_RWKV + LANES:RWKV_COLS]
    logw = -math.exp(-0.5) * _sigmoid(w0_ref[...] + _dot(jnp.tanh(wa), wup_ref[...]))
    a = _sigmoid(a0_ref[...] + _dot(wa, aup_ref[...]))
    g = _dot(_sigmoid(gc), gup_ref[...])
    if has_vres:
        pm = jnp.concatenate([misc_ref[r] for r in range(rows)], axis=0)
        (prevm,) = _shifted(bufv_ref, misc_ref, (1,))
        pmv = pm + (prevm - pm) * muv_ref[...]
        vf = jnp.concatenate([vf_ref[r] for r in range(rows)], axis=0)
        v = v + (vf - v) * _sigmoid(v0_ref[...] + _dot(pmv, vup_ref[...]))
    else:
        for r in range(rows):
            vf_out_ref[r] = v[r * c:(r + 1) * c, :]

    blk = blk_ref[...]
    kkr = k * kk_ref[...]
    kk = kkr * lax.rsqrt(_dot_exact_lhs(kkr * kkr, blk) + L2_EPS)
    k2 = k * (1.0 + (a - 1.0) * ka_ref[...])

    cw = _dot_exact_rhs(tril_ref[...], logw)
    cw_last = _last_row_of_each_chunk(cw, rows, c)
    dec_in = jnp.exp(cw)
    dec_ex = jnp.exp(cw - logw)
    dec_inv = jnp.exp(-cw)
    dec_tail = jnp.exp(cw_last - cw)
    dec_all = jnp.exp(cw_last)
    kka = kk * a
    b_p = (kk * dec_ex).astype(BF16)
    r_p = (rec * dec_in).astype(BF16)
    a_i = (-kka * dec_inv).astype(BF16)
    k_i = (k2 * dec_inv).astype(BF16)
    a_d = (-kka * dec_tail).astype(BF16)
    k_d = (k2 * dec_tail).astype(BF16)
    v_b = v.astype(BF16)

    strict, tril, _ = _time_masks(c)
    tril2 = jnp.concatenate([tril, tril], axis=1)
    probs = [(n, h, slice(n * c, (n + 1) * c), slice(h * PAIR, (h + 1) * PAIR))
             for n in range(rows) for h in range(H_RWKV // 2)]
    left = [jnp.concatenate([b_p[rs, sl], r_p[rs, sl]], axis=0) for _, _, rs, sl in probs]
    v_st = [_stack_heads(v_b[rs, sl]) for _, _, rs, sl in probs]
    m = [_dot_nt(lf, jnp.concatenate([_stack_heads(a_i[rs, sl]), _stack_heads(k_i[rs, sl])], axis=0))
         for lf, (_, _, rs, sl) in zip(left, probs)]
    x0 = [_dot_nt(lf, s_ref[n, h]) for lf, (n, h, _, _) in zip(left, probs)]
    u = [xi[0:c] + _dot(jnp.where(strict, mi[0:c, 2 * c:4 * c], 0.0), vi) for xi, mi, vi in zip(x0, m, v_st)]
    pw = [jnp.where(strict, mi[0:c, 0:2 * c], 0.0).astype(BF16) for mi in m]
    for j in range(NEUMANN_LEVELS):
        stepped = [_power_step(pi, ui, j == NEUMANN_LEVELS - 1) for pi, ui in zip(pw, u)]
        pw = [sp for sp, _ in stepped]
        u = [su for _, su in stepped]
    uv = [jnp.concatenate([_stack_heads(ui.astype(BF16)), vi], axis=0) for ui, vi in zip(u, v_st)]
    for i, (n, h, rs, sl) in enumerate(probs):
        ys_ref[rs, sl] = x0[i][c:2 * c] + _dot(jnp.where(tril2, m[i][c:2 * c, :], 0.0), uv[i])
    for i, (n, h, rs, sl) in enumerate(probs):
        tails = jnp.concatenate([_stack_heads(a_d[rs, sl]), _stack_heads(k_d[rs, sl])], axis=0)
        s_ref[n, h] = s_ref[n, h] * dec_all[(n + 1) * c - 1:(n + 1) * c, sl] + _dot_tn(uv[i], tails)
    y = ys_ref[...]

    inv_d = 1.0 / HEAD_DIM
    mean = _dot_exact_lhs(y, blk) * inv_d
    yc = y - mean
    var = _dot_exact_lhs(yc * yc, blk) * inv_d
    yn = yc * lax.rsqrt(var + RWKV_GN_EPS) * lnw_ref[...] + lnb_ref[...]
    bonus = _dot_exact_lhs(rec * k2 * rk_ref[...], blk) * v
    out = (yn + bonus) * g
    for r in range(rows):
        y_ref[r] = out[r * c:(r + 1) * c, :]


def _rwkv(p_rwkv, misc, v_first, prm, consts):
    b, t, _ = p_rwkv.shape
    c = CHUNK
    has_vres = v_first is not None
    row = lambda n: _seq_const((1, n))
    common = [prm["mu"], prm["w0"], prm["w_up"], prm["a0"], prm["a_up"], prm["g_up"], prm["k_k"], prm["k_a"],
              prm["r_k"], prm["ln_w"], prm["ln_b"]]
    common_specs = [row(RWKV_COLS), row(D_RWKV), _seq_const((LANES, D_RWKV)), row(D_RWKV),
                    _seq_const((LANES, D_RWKV)), _seq_const((RWKV_GATE_RANK, D_RWKV)), row(D_RWKV), row(D_RWKV),
                    row(D_RWKV), row(D_RWKV), row(D_RWKV)]
    rows = _seq_rows(b)
    const_in = [consts["tril"], consts["blk_rwkv"]]
    const_specs = [_seq_const(a.shape) for a in const_in]
    state = pltpu.VMEM((rows, H_RWKV // 2, PAIR, PAIR), F32)
    ys = pltpu.VMEM((rows * c, D_RWKV), F32)
    if has_vres:
        ins = [p_rwkv, misc, v_first] + common + [prm["mu_v"], prm["v0"], prm["v_up"]] + const_in
        in_specs = ([_seq_spec(c, RWKV_COLS, rows), _seq_spec(c, MISC_COLS, rows), _seq_spec(c, D_RWKV, rows)]
                    + common_specs + [row(MISC_COLS), row(D_RWKV), _seq_const((MISC_COLS, D_RWKV))] + const_specs)
        out_specs = _seq_spec(c, D_RWKV, rows)
        out_shape = jax.ShapeDtypeStruct((b, t, D_RWKV), F32)
        scratch = [pltpu.VMEM((rows, SUBLANES + c, RWKV_COLS), F32),
                   pltpu.VMEM((rows, SUBLANES + c, MISC_COLS), F32), state, ys]
    else:
        ins = [p_rwkv] + common + const_in
        in_specs = [_seq_spec(c, RWKV_COLS, rows)] + common_specs + const_specs
        out_specs = [_seq_spec(c, D_RWKV, rows)] * 2
        out_shape = [jax.ShapeDtypeStruct((b, t, D_RWKV), F32)] * 2
        scratch = [pltpu.VMEM((rows, SUBLANES + c, RWKV_COLS), F32), state, ys]
    out = pl.pallas_call(
        functools.partial(_rwkv_body, has_vres),
        grid=(b // rows, t // c),
        in_specs=in_specs, out_specs=out_specs, out_shape=out_shape, scratch_shapes=scratch,
        compiler_params=_seq_params(),
        name="rwkv7_vres" if has_vres else "rwkv7_first",
    )(*ins)
    return (out, v_first) if has_vres else (out[0], out[1])


def _ret_body(p_ref, cos_ref, sin_ref, gnw_ref, dec_ref, qw_ref, kw_ref, cd_ref, blk_ref, mask_ref, y_ref, r_ref):
    rows, c, _ = p_ref.shape

    @pl.when(pl.program_id(1) == 0)
    def _():
        r_ref[...] = jnp.zeros(r_ref.shape, F32)

    stack = lambda ref, reps: jnp.concatenate(
        [jnp.concatenate([ref[n]] * reps, axis=1) for n in range(rows)], axis=0)
    p = stack(p_ref, 1)
    q = p[:, 0:D_RET]
    k = p[:, D_RET:2 * D_RET]
    v = p[:, 2 * D_RET:3 * D_RET].astype(BF16)
    gate = p[:, 3 * D_RET:4 * D_RET]
    cos = stack(cos_ref, D_RET // LANES)
    sin = stack(sin_ref, D_RET // LANES)
    lane = lax.broadcasted_iota(jnp.int32, q.shape, 1)
    first_half = (lane % HEAD_DIM) < (HEAD_DIM // 2)
    half = HEAD_DIM // 2

    def rope(x):
        partner = jnp.where(first_half, pltpu.roll(x, D_RET - half, 1), pltpu.roll(x, half, 1))
        return x * cos + partner * sin

    q = rope(q) * HEAD_DIM ** -0.5
    k = rope(k)
    q_in = (q * jnp.concatenate([qw_ref[...]] * rows, axis=0)).astype(BF16)
    k_out = (k * jnp.concatenate([kw_ref[...]] * rows, axis=0)).astype(BF16)
    q = q.astype(BF16)
    k = k.astype(BF16)
    chunk_lane = lax.broadcasted_iota(jnp.int32, (c, D_RET), 1)
    in_head = [(chunk_lane // HEAD_DIM) == h for h in range(H_RET)]
    chunks = [(n, slice(n * c, (n + 1) * c)) for n in range(rows)]
    scores = [[(_dot_nt(jnp.where(m, q[rs], 0.0), k[rs]) * dec_ref[h]).astype(BF16) for h, m in enumerate(in_head)]
              for _, rs in chunks]
    outs = []
    for (n, rs), sc in zip(chunks, scores):
        lhs = jnp.concatenate(sc + [q_in[rs]], axis=1)
        rhs = jnp.concatenate([jnp.where(m, v[rs], 0.0) for m in in_head] + [r_ref[n].astype(BF16)], axis=0)
        outs.append(jnp.dot(lhs, rhs, preferred_element_type=F32))
    for n, rs in chunks:
        r_ref[n] = r_ref[n] * cd_ref[...] + _dot_tn(k_out[rs], v[rs]) * mask_ref[...].astype(F32)
    o = jnp.concatenate(outs, axis=0)

    blk = blk_ref[...]
    inv_d = 1.0 / HEAD_DIM
    mean = _dot_exact_lhs(o, blk) * inv_d
    oc = o - mean
    var = _dot_exact_lhs(oc * oc, blk) * inv_d
    out = _silu(gate) * (oc * lax.rsqrt(var + RET_GN_EPS) * gnw_ref[...])
    for n, rs in chunks:
        y_ref[n] = out[rs]


def _ret(p_ret, cos, sin, gn_w, consts):
    b, t, _ = p_ret.shape
    c = RET_CHUNK
    rows = _seq_rows(b)
    tables = [consts[name] for name in ("ret_decay", "ret_qw", "ret_kw", "ret_cd", "blk_ret", "ret_mask")]
    return pl.pallas_call(
        _ret_body,
        grid=(b // rows, t // c),
        in_specs=[_seq_spec(c, RET_COLS, rows), _seq_spec(c, LANES, rows), _seq_spec(c, LANES, rows),
                  _seq_const((1, D_RET))] + [_seq_const(a.shape) for a in tables],
        out_specs=_seq_spec(c, D_RET, rows),
        out_shape=jax.ShapeDtypeStruct((b, t, D_RET), F32),
        scratch_shapes=[pltpu.VMEM((rows, D_RET, D_RET), F32)],
        compiler_params=_seq_params(),
        name="retention",
    )(p_ret, cos, sin, gn_w, *tables)


def _gdn_body(p_ref, misc_ref, convw_ref, alog_ref, dtb_ref, nw_ref, tril_ref, blk_ref, eg_ref, eb_ref, el_ref,
              y_ref, buf_ref, s_ref, os_ref):
    rows, c, _ = p_ref.shape

    @pl.when(pl.program_id(1) == 0)
    def _():
        s_ref[...] = jnp.zeros(s_ref.shape, F32)
        _zero_tails(buf_ref)

    gate = jnp.concatenate([p_ref[r, :, 3 * D_GDN:4 * D_GDN] for r in range(rows)], axis=0)
    taps = _shifted(buf_ref, p_ref, tuple(GDN_CONV - 1 - j for j in range(GDN_CONV)))
    conv = taps[0] * convw_ref[0:1, :]
    for j in range(1, GDN_CONV):
        conv = conv + taps[j] * convw_ref[j:j + 1, :]
    qkv = _silu(conv)
    blk = blk_ref[...]
    q = qkv[:, 0:D_GDN]
    k = qkv[:, D_GDN:2 * D_GDN]
    v = qkv[:, 2 * D_GDN:3 * D_GDN]
    q = q * lax.rsqrt(_dot_exact_lhs(q * q, blk) + L2_EPS) * HEAD_DIM ** -0.5
    k = k * lax.rsqrt(_dot_exact_lhs(k * k, blk) + L2_EPS)

    m = jnp.concatenate([misc_ref[r] for r in range(rows)], axis=0)
    beta = _sigmoid(m)
    g = -jnp.exp(alog_ref[...]) * _softplus(m + dtb_ref[...])
    gcs = _dot_exact_rhs(tril_ref[...], g)
    g_wide = _dot_exact_lhs(gcs, eg_ref[...])
    g_lane = _dot_exact_lhs(gcs, el_ref[...])
    b_lane = _dot_exact_lhs(beta, eb_ref[...])
    g_end = _last_row_of_each_chunk(g_lane, rows, c)
    eg = jnp.exp(g_lane)
    kb = k * b_lane
    left = jnp.concatenate([kb.astype(BF16), q.astype(BF16)], axis=1)
    rhs_uw = jnp.concatenate([(v * b_lane).astype(BF16), (kb * eg).astype(BF16)], axis=1)
    q_e = (q * eg).astype(BF16)
    k_tail = (k * jnp.exp(g_end - g_lane)).astype(BF16)
    s_decay = jnp.exp(g_end)
    k_b16 = k.astype(BF16)

    strict, tril, diag = _time_masks(c)
    eye = jnp.where(diag, 1.0, 0.0)
    probs = [(n, h, slice(n * c, (n + 1) * c), slice(h * PAIR, (h + 1) * PAIR))
             for n in range(rows) for h in range(H_GDN // 2)]
    qk, decay = [], []
    for n, h, rs, sl in probs:
        kq = jnp.concatenate([left[rs, sl], left[rs, D_GDN + h * PAIR:D_GDN + (h + 1) * PAIR]], axis=0)
        qk.append(_dot_nt(kq, _stack_heads(k_b16[rs, sl])))
        w0, w1 = 2 * h * LANES, (2 * h + 1) * LANES
        g_cols = jnp.concatenate([g_wide[rs, w0:w0 + LANES], g_wide[rs, w1:w1 + LANES]], axis=0)
        g_rows = g_cols.T[0:c, :]
        decay.append(jnp.where(tril, jnp.exp(jnp.where(tril, g_lane[rs, sl] - g_rows, 0.0)), 0.0))
    lmat = [jnp.where(strict, qk_[0:c] * d_, 0.0) for qk_, d_ in zip(qk, decay)]
    tm = [eye - l_ for l_ in lmat]
    pw = [l_.astype(BF16) for l_ in lmat]
    pw = [jnp.dot(p_, _stack_heads(p_), preferred_element_type=F32).astype(BF16) for p_ in pw]
    for j in range(1, NEUMANN_LEVELS):
        stepped = [_power_step(p_, t_, j == NEUMANN_LEVELS - 1) for p_, t_ in zip(pw, tm)]
        pw = [sp for sp, _ in stepped]
        tm = [st for _, st in stepped]
    uw = [_dot(t_, _stack_heads(jnp.concatenate([rhs_uw[rs, sl], rhs_uw[rs, D_GDN + h * PAIR:D_GDN + (h + 1) * PAIR]],
                                                axis=1)))
          for t_, (_, h, rs, sl) in zip(tm, probs)]
    v_new = [_stack_heads((uw_[:, 0:LANES] - _dot(uw_[:, LANES:2 * LANES], s_ref[n, h])).astype(BF16))
             for uw_, (n, h, _, _) in zip(uw, probs)]
    for i, (n, h, rs, sl) in enumerate(probs):
        attn = (qk[i][c:2 * c] * decay[i]).astype(BF16)
        os_ref[rs, sl] = jnp.dot(jnp.concatenate([q_e[rs, sl], attn], axis=1),
                                 jnp.concatenate([s_ref[n, h].astype(BF16), v_new[i]], axis=0),
                                 preferred_element_type=F32)
    for i, (n, h, rs, sl) in enumerate(probs):
        s_ref[n, h] = (s_ref[n, h] * s_decay[(n + 1) * c - 1:(n + 1) * c, sl]
                       + _dot_tn(_stack_heads(k_tail[rs, sl]), v_new[i]))
    o = os_ref[...]
    ms = _dot_exact_lhs(o * o, blk) * (1.0 / HEAD_DIM)
    out = o * lax.rsqrt(ms + NORM_EPS) * nw_ref[...] * _silu(gate)
    for r in range(rows):
        y_ref[r] = out[r * c:(r + 1) * c, :]


def _gdn(p_gdn, misc, prm, consts):
    b, t, _ = p_gdn.shape
    c = CHUNK
    rows = _seq_rows(b)
    return pl.pallas_call(
        _gdn_body,
        grid=(b // rows, t // c),
        in_specs=[_seq_spec(c, GDN_MAIN_COLS, rows), _seq_spec(c, MISC_COLS, rows),
                  _seq_const((GDN_CONV, 3 * D_GDN)),
                  _seq_const((1, MISC_COLS)), _seq_const((1, MISC_COLS)), _seq_const((1, D_GDN)),
                  ] + [_seq_const(consts[name].shape) for name in ("tril", "blk_gdn", "exp_g", "exp_b", "exp_l")],
        out_specs=_seq_spec(c, D_GDN, rows),
        out_shape=jax.ShapeDtypeStruct((b, t, D_GDN), F32),
        scratch_shapes=[pltpu.VMEM((rows, SUBLANES + c, 3 * D_GDN), F32),
                        pltpu.VMEM((rows, H_GDN // 2, PAIR, PAIR), F32),
                        pltpu.VMEM((rows * c, D_GDN), F32)],
        compiler_params=_seq_params(),
        name="gated_deltanet",
    )(p_gdn, misc, prm["conv_w"], prm["a_log"], prm["dt_bias"], prm["norm_w"], consts["tril"], consts["blk_gdn"],
      consts["exp_g"], consts["exp_b"], consts["exp_l"])


def _block_ones(n):
    i = jnp.arange(n) // HEAD_DIM
    return (i[:, None] == i[None, :]).astype(BF16)


def _constants(rows):
    c = CHUNK
    idx = jnp.arange(rows * c)
    same_chunk = (idx[:, None] // c) == (idx[None, :] // c)
    tile_k = lambda m, parts: jnp.tile(m, (parts, 1))
    consts = {"tril": jnp.tile((same_chunk & (idx[:, None] >= idx[None, :])).astype(BF16), (1, SUM_PARTS)),
              "blk_rwkv": tile_k(_block_ones(D_RWKV), NORM_PARTS), "blk_ret": tile_k(_block_ones(D_RET), NORM_PARTS),
              "blk_gdn": tile_k(_block_ones(D_GDN), NORM_PARTS), "ret_mask": _block_ones(D_RET)}
    rc = RET_CHUNK
    ridx = jnp.arange(rc)
    log_gamma = jnp.log(1.0 - 2.0 ** (-5.0 - jnp.arange(H_RET, dtype=F32)))
    rel = ridx[:, None] - ridx[None, :]
    consts["ret_decay"] = jnp.where(rel >= 0, jnp.exp(log_gamma[:, None, None] * jnp.maximum(rel, 0)), 0.0)
    lanes_gamma = jnp.repeat(log_gamma, HEAD_DIM)[None, :]
    consts["ret_qw"] = jnp.exp(lanes_gamma * (ridx[:, None] + 1))
    consts["ret_kw"] = jnp.exp(lanes_gamma * (rc - 1 - ridx[:, None]))
    consts["ret_cd"] = jnp.exp(lanes_gamma * rc)
    rows = jnp.arange(MISC_COLS)[:, None]
    wide_head = jnp.arange(H_GDN * LANES)[None, :] // LANES
    lane_head = jnp.arange(D_GDN)[None, :] // HEAD_DIM
    consts["exp_g"] = tile_k((rows == MISC_ALOG + wide_head).astype(BF16), SUM_PARTS)
    consts["exp_b"] = tile_k((rows == MISC_BETA + lane_head).astype(BF16), NORM_PARTS)
    consts["exp_l"] = tile_k((rows == MISC_ALOG + lane_head).astype(BF16), SUM_PARTS)
    return consts


def _pad_rows(w, top, total):
    return jnp.pad(w, ((top, total - top - w.shape[0]), (0, 0)))


def _pad_lanes(v, left, total):
    return jnp.pad(v, (left, total - left - v.shape[0]))[None, :]


def kernel(x, positions, norm_g, ffn_w_gate, ffn_w_up, ffn_w_down, w_in_first, w_in_rest, w_out, rwkv_mu, rwkv_w0, rwkv_w_up, rwkv_a0, rwkv_a_up, rwkv_g_up, rwkv_k_k, rwkv_k_a, rwkv_r_k, rwkv_ln_w, rwkv_ln_b, rwkv_mu_vres, rwkv_v0, rwkv_v_up, ret_gn_w, gdn_conv_w, gdn_A_log, gdn_dt_bias, gdn_norm_w):
    b, t, d = x.shape
    depth = norm_g.shape[0]
    n = b * t
    assert d == D_MODEL and n % ROW_TILE == 0 and t % RET_CHUNK == 0 and t % CHUNK == 0
    consts = _constants(_seq_rows(b))
    cos, sin = _rope_tables(positions)
    x = x.reshape(n, d)
    v_first = None
    wg, wu, wd, wo = (w.astype(BF16) for w in (ffn_w_gate, ffn_w_up, ffn_w_down, w_out))
    for l in range(depth):
        ng = norm_g[l][:, None, :]
        x = _ffn(x, ng[0], ng[1], wg, wu, wd, l, 0)

        w_in = w_in_first if l == 0 else w_in_rest[l - 1]
        gdn_small = w_in[:, RWKV_COLS + RET_COLS + GDN_MAIN_COLS:IN_COLS_FIRST]
        vres_cols = w_in[:, IN_COLS_FIRST:] if l > 0 else jnp.zeros((d, RWKV_VRES_RANK), F32)
        misc_w = jnp.concatenate([gdn_small, vres_cols], axis=1)
        misc_w = jnp.pad(misc_w, ((0, 0), (0, MISC_COLS - misc_w.shape[1])))
        w_packed = jnp.concatenate([w_in[:, :RWKV_COLS + RET_COLS + GDN_MAIN_COLS], misc_w], axis=1).astype(BF16)
        p_rwkv, p_ret, p_gdn, p_misc = _inproj(x, ng[2], w_packed)
        p_rwkv = p_rwkv.reshape(b, t, RWKV_COLS)
        p_ret = p_ret.reshape(b, t, RET_COLS)
        p_gdn = p_gdn.reshape(b, t, GDN_MAIN_COLS)
        p_misc = p_misc.reshape(b, t, MISC_COLS)

        rw = {"mu": rwkv_mu[l][None, :], "w0": rwkv_w0[l][None, :],
              "w_up": _pad_rows(rwkv_w_up[l], 0, LANES).astype(BF16), "a0": rwkv_a0[l][None, :],
              "a_up": _pad_rows(rwkv_a_up[l], RWKV_DECAY_RANK, LANES).astype(BF16),
              "g_up": rwkv_g_up[l].astype(BF16), "k_k": rwkv_k_k[l][None, :], "k_a": rwkv_k_a[l][None, :],
              "r_k": rwkv_r_k[l][None, :], "ln_w": rwkv_ln_w[l][None, :], "ln_b": rwkv_ln_b[l][None, :]}
        if l > 0:
            rw["mu_v"] = _pad_lanes(rwkv_mu_vres[l - 1], MISC_VRES, MISC_COLS)
            rw["v0"] = rwkv_v0[l - 1][None, :]
            rw["v_up"] = _pad_rows(rwkv_v_up[l - 1], MISC_VRES, MISC_COLS).astype(BF16)
        y_a, v_first = _rwkv(p_rwkv, p_misc, v_first, rw, consts)
        y_b = _ret(p_ret, cos, sin, ret_gn_w[l][None, :], consts)
        gd = {"conv_w": gdn_conv_w[l], "a_log": _pad_lanes(gdn_A_log[l], MISC_ALOG, MISC_COLS),
              "dt_bias": _pad_lanes(gdn_dt_bias[l], MISC_ALOG, MISC_COLS),
              "norm_w": jnp.tile(gdn_norm_w[l], H_GDN)[None, :]}
        y_c = _gdn(p_gdn, p_misc, gd, consts)

        mixer = (y_a.reshape(n, D_RWKV), y_b.reshape(n, D_RET), y_c.reshape(n, D_GDN), wo, ng[3])
        x = _ffn(x, ng[4], ng[5], wg, wu, wd, l, 1, mixer)
    return x.reshape(b, t, d)
```

```python
import functools
import math

import jax
import jax.numpy as jnp
from jax import lax
from jax.experimental import pallas as pl
from jax.experimental.pallas import tpu as pltpu

F32 = jnp.float32
BF16 = jnp.bfloat16

D_MODEL = 1024
D_FF = 2816
HEAD_DIM = 64
H_RWKV, H_RET, H_GDN = 6, 4, 6
D_RWKV, D_RET, D_GDN = H_RWKV * HEAD_DIM, H_RET * HEAD_DIM, H_GDN * HEAD_DIM
RWKV_DECAY_RANK, RWKV_ICLR_RANK, RWKV_VRES_RANK, RWKV_GATE_RANK = 64, 64, 32, 128
RWKV_COLS = 3 * D_RWKV + RWKV_DECAY_RANK + RWKV_ICLR_RANK + RWKV_GATE_RANK
RET_COLS = 4 * D_RET
GDN_MAIN_COLS = 4 * D_GDN
GDN_COLS = GDN_MAIN_COLS + 2 * H_GDN
IN_COLS_FIRST = RWKV_COLS + RET_COLS + GDN_COLS
GDN_CONV = 4
NORM_EPS = 1e-6
L2_EPS = 1e-6
RWKV_GN_EPS = HEAD_DIM * 1e-5
RET_GN_EPS = 1e-5
ROPE_THETA = 10000.0

LANES = 128
SUBLANES = 8
MISC_COLS = LANES
MISC_BETA, MISC_ALOG, MISC_VRES = 0, H_GDN, 2 * H_GDN
CHUNK = 64
RET_CHUNK = 128
PAIR = 2 * HEAD_DIM
SEQ_ROWS = 8
ROW_TILE = 512
FFN_ROW_TILE = 1024
FF_CHUNK = 512
SUM_PARTS = 3
NORM_PARTS = 1
VMEM_LIMIT = 56 * 1024 * 1024


def _dot(a, b):
    return jnp.dot(a.astype(BF16), b.astype(BF16), preferred_element_type=F32)


def _dot_nt(a, b):
    return lax.dot_general(a.astype(BF16), b.astype(BF16), (((1,), (1,)), ((), ())),
                           preferred_element_type=F32)


def _dot_tn(a, b):
    return lax.dot_general(a.astype(BF16), b.astype(BF16), (((0,), (0,)), ((), ())),
                           preferred_element_type=F32)


def _split(x, parts):
    out = []
    for _ in range(parts - 1):
        h = x.astype(BF16)
        out.append(h)
        x = x - h.astype(F32)
    out.append(x.astype(BF16))
    return out


def _dot_exact_rhs(a_tiled, x):
    parts = a_tiled.shape[1] // x.shape[0]
    return jnp.dot(a_tiled, jnp.concatenate(_split(x, parts), axis=0), preferred_element_type=F32)


def _dot_exact_lhs(x, b_tiled):
    parts = b_tiled.shape[0] // x.shape[1]
    return jnp.dot(jnp.concatenate(_split(x, parts), axis=1), b_tiled, preferred_element_type=F32)


def _rms(x, g):
    return x * lax.rsqrt(jnp.mean(x * x, axis=-1, keepdims=True) + NORM_EPS) * g


def _sigmoid(x):
    return 1.0 / (1.0 + jnp.exp(-x))


def _silu(x):
    return x * _sigmoid(x)


def _softplus(x):
    return jnp.maximum(x, 0.0) + jnp.log(1.0 + jnp.exp(-jnp.abs(x)))


def _stack_heads(x):
    lane = lax.broadcasted_iota(jnp.int32, x.shape, 1) % PAIR
    return jnp.concatenate([jnp.where(lane < HEAD_DIM, x, 0.0), jnp.where(lane >= HEAD_DIM, x, 0.0)], axis=0)


def _time_masks(c):
    i = lax.broadcasted_iota(jnp.int32, (c, 2 * c), 0)
    j = lax.broadcasted_iota(jnp.int32, (c, 2 * c), 1) % c
    return i > j, i >= j, i == j


assert CHUNK == HEAD_DIM
NEUMANN_LEVELS = int(math.log2(CHUNK))


def _power_step(p, x, last):
    n = p.shape[1]
    if last:
        return None, x + jnp.dot(p, _stack_heads(x.astype(BF16)), preferred_element_type=F32)
    z = jnp.dot(p, _stack_heads(jnp.concatenate([p, x.astype(BF16)], axis=1)), preferred_element_type=F32)
    return z[:, :n].astype(BF16), x + z[:, n:]


def _ffn_body(after_mixer, *refs):
    if after_mixer:
        (x_ref, ya_ref, yb_ref, yc_ref, wo_ref, gmix_ref,
         gpre_ref, gpost_ref, wg_ref, wu_ref, wd_ref, o_ref, h_ref, a_ref) = refs
        y_all = jnp.concatenate([r[...].astype(BF16) for r in (ya_ref, yb_ref, yc_ref)], axis=1)
        x = x_ref[...] + _rms(jnp.dot(y_all, wo_ref[...], preferred_element_type=F32), gmix_ref[...])
    else:
        x_ref, gpre_ref, gpost_ref, wg_ref, wu_ref, wd_ref, o_ref, h_ref, a_ref = refs
        x = x_ref[...]
    h_ref[...] = _rms(x, gpre_ref[...]).astype(BF16)
    for c0 in range(0, D_FF, FF_CHUNK):
        w = min(FF_CHUNK, D_FF - c0)
        h = h_ref[...]
        g = jnp.dot(h, wg_ref[:, c0:c0 + w], preferred_element_type=F32)
        u = jnp.dot(h, wu_ref[:, c0:c0 + w], preferred_element_type=F32)
        a_ref[:, c0:c0 + w] = (_silu(g) * u).astype(BF16)
    y = jnp.dot(a_ref[...], wd_ref[...], preferred_element_type=F32)
    o_ref[...] = x + 0.5 * _rms(y, gpost_ref[...])


def _row_spec(width, tile=ROW_TILE):
    return pl.BlockSpec((tile, width), lambda i: (i, 0))


def _const_spec(shape):
    zeros = (0,) * len(shape)
    return pl.BlockSpec(shape, lambda *_: zeros, pipeline_mode=pl.Buffered(1))


def _dense_params():
    return pltpu.CompilerParams(dimension_semantics=("parallel",), vmem_limit_bytes=VMEM_LIMIT)


def _layer_spec(shape, *lead):
    return pl.BlockSpec((None,) * len(lead) + shape, lambda *_: lead + (0,) * len(shape),
                        pipeline_mode=pl.Buffered(1))


def _ffn(x, g_pre, g_post, wg, wu, wd, layer, half, mixer=None):
    n = x.shape[0]
    tile = math.gcd(n, FFN_ROW_TILE)
    ins, in_specs = [x], [_row_spec(D_MODEL, tile)]
    if mixer is not None:
        ya, yb, yc, wo, g_mix = mixer
        ins += [ya, yb, yc, wo, g_mix]
        in_specs += [_row_spec(D_RWKV, tile), _row_spec(D_RET, tile), _row_spec(D_GDN, tile),
                     _layer_spec((D_RWKV + D_RET + D_GDN, D_MODEL), layer), _const_spec((1, D_MODEL))]
    ins += [g_pre, g_post, wg, wu, wd]
    in_specs += [_const_spec((1, D_MODEL)), _const_spec((1, D_MODEL)),
                 _layer_spec((D_MODEL, D_FF), layer, half), _layer_spec((D_MODEL, D_FF), layer, half),
                 _layer_spec((D_FF, D_MODEL), layer, half)]
    return pl.pallas_call(
        functools.partial(_ffn_body, mixer is not None),
        grid=(n // tile,),
        in_specs=in_specs,
        out_specs=_row_spec(D_MODEL, tile),
        out_shape=jax.ShapeDtypeStruct((n, D_MODEL), F32),
        scratch_shapes=[pltpu.VMEM((tile, D_MODEL), BF16), pltpu.VMEM((tile, D_FF), BF16)],
        compiler_params=_dense_params(),
        name="ffn_after_mixer" if mixer is not None else "ffn",
    )(*ins)


_IN_WIDTHS = (RWKV_COLS, RET_COLS, GDN_MAIN_COLS, MISC_COLS)
IN_COLS_PACKED = sum(_IN_WIDTHS)


def _inproj_body(x_ref, g_ref, w_ref, o_rwkv, o_ret, o_gdn, o_misc, h_ref):
    h_ref[...] = _rms(x_ref[...], g_ref[...]).astype(BF16)
    off = 0
    for o_ref in (o_rwkv, o_ret, o_gdn, o_misc):
        width = o_ref.shape[-1]
        for c0 in range(0, width, FF_CHUNK):
            w = min(FF_CHUNK, width - c0)
            o_ref[:, c0:c0 + w] = jnp.dot(h_ref[...], w_ref[:, off + c0:off + c0 + w],
                                          preferred_element_type=F32)
        off += width


def _inproj(x, g, w):
    n = x.shape[0]
    return pl.pallas_call(
        _inproj_body,
        grid=(n // ROW_TILE,),
        in_specs=[_row_spec(D_MODEL), _const_spec((1, D_MODEL)), _const_spec((D_MODEL, IN_COLS_PACKED))],
        out_specs=[_row_spec(w_) for w_ in _IN_WIDTHS],
        out_shape=[jax.ShapeDtypeStruct((n, w_), F32) for w_ in _IN_WIDTHS],
        scratch_shapes=[pltpu.VMEM((ROW_TILE, D_MODEL), BF16)],
        compiler_params=_dense_params(),
        name="inproj",
    )(x, g, w)


def _rope_body(pos_ref, invf_ref, sign_ref, cos_ref, sin_ref):
    ang = pos_ref[0] * invf_ref[...]
    cos_ref[0] = jnp.cos(ang)
    sin_ref[0] = jnp.sin(ang) * sign_ref[...]


def _rope_tables(positions):
    b, t = positions.shape
    half = HEAD_DIM // 2
    inv_freq = ROPE_THETA ** (-jnp.arange(0, HEAD_DIM, 2, dtype=F32) / HEAD_DIM)
    invf = jnp.tile(inv_freq, LANES // half)[None, :]
    sign = jnp.tile(jnp.concatenate([-jnp.ones((half,), F32), jnp.ones((half,), F32)]), LANES // HEAD_DIM)[None, :]
    pos = positions.astype(F32)[..., None]
    tt = min(t, 512)
    return pl.pallas_call(
        _rope_body,
        grid=(b, t // tt),
        in_specs=[pl.BlockSpec((1, tt, 1), lambda i, j: (i, j, 0)),
                  pl.BlockSpec((1, LANES), lambda i, j: (0, 0)),
                  pl.BlockSpec((1, LANES), lambda i, j: (0, 0))],
        out_specs=[pl.BlockSpec((1, tt, LANES), lambda i, j: (i, j, 0))] * 2,
        out_shape=[jax.ShapeDtypeStruct((b, t, LANES), F32)] * 2,
        compiler_params=pltpu.CompilerParams(dimension_semantics=("parallel", "parallel")),
        name="rope_tables",
    )(pos, invf, sign)


def _seq_spec(c, width, rows=1):
    return pl.BlockSpec((rows, c, width), lambda b, t: (b, t, 0))


def _seq_const(shape):
    zeros = (0,) * len(shape)
    return pl.BlockSpec(shape, lambda b, t: zeros)


def _seq_rows(b):
    return math.gcd(b, SEQ_ROWS)


def _seq_params():
    return pltpu.CompilerParams(dimension_semantics=("parallel", "arbitrary"), vmem_limit_bytes=VMEM_LIMIT)


def _shifted(buf_ref, x_ref, shifts):
    rows, c, _ = x_ref.shape
    width = buf_ref.shape[-1]
    out = [[] for _ in shifts]
    for r in range(rows):
        x = x_ref[r, :, 0:width]
        buf_ref[r, SUBLANES:SUBLANES + c, :] = x
        for i, s in enumerate(shifts):
            out[i].append(buf_ref[r, SUBLANES - s:SUBLANES - s + c, :])
        buf_ref[r, 0:SUBLANES, :] = x[c - SUBLANES:c, :]
    return [jnp.concatenate(o, axis=0) for o in out]


def _zero_tails(buf_ref):
    rows, _, width = buf_ref.shape
    buf_ref[:, 0:SUBLANES, :] = jnp.zeros((rows, SUBLANES, width), F32)


def _last_row_of_each_chunk(x, rows, c):
    return jnp.concatenate([jnp.broadcast_to(x[(r + 1) * c - 1:(r + 1) * c, :], (c, x.shape[1]))
                            for r in range(rows)], axis=0)


def _rwkv_body(has_vres, *refs):
    if has_vres:
        (p_ref, misc_ref, vf_ref, mu_ref, w0_ref, wup_ref, a0_ref, aup_ref, gup_ref, kk_ref, ka_ref, rk_ref,
         lnw_ref, lnb_ref, muv_ref, v0_ref, vup_ref, tril_ref, blk_ref,
         y_ref, buf_ref, bufv_ref, s_ref, ys_ref) = refs
    else:
        (p_ref, mu_ref, w0_ref, wup_ref, a0_ref, aup_ref, gup_ref, kk_ref, ka_ref, rk_ref,
         lnw_ref, lnb_ref, tril_ref, blk_ref,
         y_ref, vf_out_ref, buf_ref, s_ref, ys_ref) = refs
        bufv_ref = None
    rows, c, _ = p_ref.shape

    @pl.when(pl.program_id(1) == 0)
    def _():
        s_ref[...] = jnp.zeros(s_ref.shape, F32)
        _zero_tails(buf_ref)
        if bufv_ref is not None:
            _zero_tails(bufv_ref)

    p = jnp.concatenate([p_ref[r] for r in range(rows)], axis=0)
    (prev,) = _shifted(buf_ref, p_ref, (1,))
    ps = p + (prev - p) * mu_ref[...]
    rec = ps[:, 0:D_RWKV]
    k = ps[:, D_RWKV:2 * D_RWKV]
    v = ps[:, 2 * D_RWKV:3 * D_RWKV]
    wa = ps[:, 3 * D_RWKV:3 * D_RWKV + LANES]
    gc = ps[:, 3 * D_RWKV + LANES:RWKV_COLS]
    logw = -math.exp(-0.5) * _sigmoid(w0_ref[...] + _dot(jnp.tanh(wa), wup_ref[...]))
    a = _sigmoid(a0_ref[...] + _dot(wa, aup_ref[...]))
    g = _dot(_sigmoid(gc), gup_ref[...])
    if has_vres:
        pm = jnp.concatenate([misc_ref[r] for r in range(rows)], axis=0)
        (prevm,) = _shifted(bufv_ref, misc_ref, (1,))
        pmv = pm + (prevm - pm) * muv_ref[...]
        vf = jnp.concatenate([vf_ref[r] for r in range(rows)], axis=0)
        v = v + (vf - v) * _sigmoid(v0_ref[...] + _dot(pmv, vup_ref[...]))
    else:
        for r in range(rows):
            vf_out_ref[r] = v[r * c:(r + 1) * c, :]

    blk = blk_ref[...]
    kkr = k * kk_ref[...]
    kk = kkr * lax.rsqrt(_dot_exact_lhs(kkr * kkr, blk) + L2_EPS)
    k2 = k * (1.0 + (a - 1.0) * ka_ref[...])

    cw = _dot_exact_rhs(tril_ref[...], logw)
    cw_last = _last_row_of_each_chunk(cw, rows, c)
    dec_in = jnp.exp(cw)
    dec_ex = jnp.exp(cw - logw)
    dec_inv = jnp.exp(-cw)
    dec_tail = jnp.exp(cw_last - cw)
    dec_all = jnp.exp(cw_last)
    kka = kk * a
    b_p = (kk * dec_ex).astype(BF16)
    r_p = (rec * dec_in).astype(BF16)
    a_i = (-kka * dec_inv).astype(BF16)
    k_i = (k2 * dec_inv).astype(BF16)
    a_d = (-kka * dec_tail).astype(BF16)
    k_d = (k2 * dec_tail).astype(BF16)
    v_b = v.astype(BF16)

    strict, tril, _ = _time_masks(c)
    tril2 = jnp.concatenate([tril, tril], axis=1)
    probs = [(n, h, slice(n * c, (n + 1) * c), slice(h * PAIR, (h + 1) * PAIR))
             for n in range(rows) for h in range(H_RWKV // 2)]
    left = [jnp.concatenate([b_p[rs, sl], r_p[rs, sl]], axis=0) for _, _, rs, sl in probs]
    v_st = [_stack_heads(v_b[rs, sl]) for _, _, rs, sl in probs]
    m = [_dot_nt(lf, jnp.concatenate([_stack_heads(a_i[rs, sl]), _stack_heads(k_i[rs, sl])], axis=0))
         for lf, (_, _, rs, sl) in zip(left, probs)]
    x0 = [_dot_nt(lf, s_ref[n, h]) for lf, (n, h, _, _) in zip(left, probs)]
    u = [xi[0:c] + _dot(jnp.where(strict, mi[0:c, 2 * c:4 * c], 0.0), vi) for xi, mi, vi in zip(x0, m, v_st)]
    pw = [jnp.where(strict, mi[0:c, 0:2 * c], 0.0).astype(BF16) for mi in m]
    for j in range(NEUMANN_LEVELS):
        stepped = [_power_step(pi, ui, j == NEUMANN_LEVELS - 1) for pi, ui in zip(pw, u)]
        pw = [sp for sp, _ in stepped]
        u = [su for _, su in stepped]
    uv = [jnp.concatenate([_stack_heads(ui.astype(BF16)), vi], axis=0) for ui, vi in zip(u, v_st)]
    for i, (n, h, rs, sl) in enumerate(probs):
        ys_ref[rs, sl] = x0[i][c:2 * c] + _dot(jnp.where(tril2, m[i][c:2 * c, :], 0.0), uv[i])
    for i, (n, h, rs, sl) in enumerate(probs):
        tails = jnp.concatenate([_stack_heads(a_d[rs, sl]), _stack_heads(k_d[rs, sl])], axis=0)
        s_ref[n, h] = s_ref[n, h] * dec_all[(n + 1) * c - 1:(n + 1) * c, sl] + _dot_tn(uv[i], tails)
    y = ys_ref[...]

    inv_d = 1.0 / HEAD_DIM
    mean = _dot_exact_lhs(y, blk) * inv_d
    yc = y - mean
    var = _dot_exact_lhs(yc * yc, blk) * inv_d
    yn = yc * lax.rsqrt(var + RWKV_GN_EPS) * lnw_ref[...] + lnb_ref[...]
    bonus = _dot_exact_lhs(rec * k2 * rk_ref[...], blk) * v
    out = (yn + bonus) * g
    for r in range(rows):
        y_ref[r] = out[r * c:(r + 1) * c, :]


def _rwkv(p_rwkv, misc, v_first, prm, consts):
    b, t, _ = p_rwkv.shape
    c = CHUNK
    has_vres = v_first is not None
    row = lambda n: _seq_const((1, n))
    common = [prm["mu"], prm["w0"], prm["w_up"], prm["a0"], prm["a_up"], prm["g_up"], prm["k_k"], prm["k_a"],
              prm["r_k"], prm["ln_w"], prm["ln_b"]]
    common_specs = [row(RWKV_COLS), row(D_RWKV), _seq_const((LANES, D_RWKV)), row(D_RWKV),
                    _seq_const((LANES, D_RWKV)), _seq_const((RWKV_GATE_RANK, D_RWKV)), row(D_RWKV), row(D_RWKV),
                    row(D_RWKV), row(D_RWKV), row(D_RWKV)]
    rows = _seq_rows(b)
    const_in = [consts["tril"], consts["blk_rwkv"]]
    const_specs = [_seq_const(a.shape) for a in const_in]
    state = pltpu.VMEM((rows, H_RWKV // 2, PAIR, PAIR), F32)
    ys = pltpu.VMEM((rows * c, D_RWKV), F32)
    if has_vres:
        ins = [p_rwkv, misc, v_first] + common + [prm["mu_v"], prm["v0"], prm["v_up"]] + const_in
        in_specs = ([_seq_spec(c, RWKV_COLS, rows), _seq_spec(c, MISC_COLS, rows), _seq_spec(c, D_RWKV, rows)]
                    + common_specs + [row(MISC_COLS), row(D_RWKV), _seq_const((MISC_COLS, D_RWKV))] + const_specs)
        out_specs = _seq_spec(c, D_RWKV, rows)
        out_shape = jax.ShapeDtypeStruct((b, t, D_RWKV), F32)
        scratch = [pltpu.VMEM((rows, SUBLANES + c, RWKV_COLS), F32),
                   pltpu.VMEM((rows, SUBLANES + c, MISC_COLS), F32), state, ys]
    else:
        ins = [p_rwkv] + common + const_in
        in_specs = [_seq_spec(c, RWKV_COLS, rows)] + common_specs + const_specs
        out_specs = [_seq_spec(c, D_RWKV, rows)] * 2
        out_shape = [jax.ShapeDtypeStruct((b, t, D_RWKV), F32)] * 2
        scratch = [pltpu.VMEM((rows, SUBLANES + c, RWKV_COLS), F32), state, ys]
    out = pl.pallas_call(
        functools.partial(_rwkv_body, has_vres),
        grid=(b // rows, t // c),
        in_specs=in_specs, out_specs=out_specs, out_shape=out_shape, scratch_shapes=scratch,
        compiler_params=_seq_params(),
        name="rwkv7_vres" if has_vres else "rwkv7_first",
    )(*ins)
    return (out, v_first) if has_vres else (out[0], out[1])


def _ret_body(p_ref, cos_ref, sin_ref, gnw_ref, dec_ref, qw_ref, kw_ref, cd_ref, blk_ref, mask_ref, y_ref, r_ref):
    rows, c, _ = p_ref.shape

    @pl.when(pl.program_id(1) == 0)
    def _():
        r_ref[...] = jnp.zeros(r_ref.shape, F32)

    stack = lambda ref, reps: jnp.concatenate(
        [jnp.concatenate([ref[n]] * reps, axis=1) for n in range(rows)], axis=0)
    p = stack(p_ref, 1)
    q = p[:, 0:D_RET]
    k = p[:, D_RET:2 * D_RET]
    v = p[:, 2 * D_RET:3 * D_RET].astype(BF16)
    gate = p[:, 3 * D_RET:4 * D_RET]
    cos = stack(cos_ref, D_RET // LANES)
    sin = stack(sin_ref, D_RET // LANES)
    lane = lax.broadcasted_iota(jnp.int32, q.shape, 1)
    first_half = (lane % HEAD_DIM) < (HEAD_DIM // 2)
    half = HEAD_DIM // 2

    def rope(x):
        partner = jnp.where(first_half, pltpu.roll(x, D_RET - half, 1), pltpu.roll(x, half, 1))
        return x * cos + partner * sin

    q = rope(q) * HEAD_DIM ** -0.5
    k = rope(k)
    q_in = (q * jnp.concatenate([qw_ref[...]] * rows, axis=0)).astype(BF16)
    k_out = (k * jnp.concatenate([kw_ref[...]] * rows, axis=0)).astype(BF16)
    q = q.astype(BF16)
    k = k.astype(BF16)
    chunk_lane = lax.broadcasted_iota(jnp.int32, (c, D_RET), 1)
    in_head = [(chunk_lane // HEAD_DIM) == h for h in range(H_RET)]
    chunks = [(n, slice(n * c, (n + 1) * c)) for n in range(rows)]
    scores = [[(_dot_nt(jnp.where(m, q[rs], 0.0), k[rs]) * dec_ref[h]).astype(BF16) for h, m in enumerate(in_head)]
              for _, rs in chunks]
    outs = []
    for (n, rs), sc in zip(chunks, scores):
        lhs = jnp.concatenate(sc + [q_in[rs]], axis=1)
        rhs = jnp.concatenate([jnp.where(m, v[rs], 0.0) for m in in_head] + [r_ref[n].astype(BF16)], axis=0)
        outs.append(jnp.dot(lhs, rhs, preferred_element_type=F32))
    for n, rs in chunks:
        r_ref[n] = r_ref[n] * cd_ref[...] + _dot_tn(k_out[rs], v[rs]) * mask_ref[...].astype(F32)
    o = jnp.concatenate(outs, axis=0)

    blk = blk_ref[...]
    inv_d = 1.0 / HEAD_DIM
    mean = _dot_exact_lhs(o, blk) * inv_d
    oc = o - mean
    var = _dot_exact_lhs(oc * oc, blk) * inv_d
    out = _silu(gate) * (oc * lax.rsqrt(var + RET_GN_EPS) * gnw_ref[...])
    for n, rs in chunks:
        y_ref[n] = out[rs]


def _ret(p_ret, cos, sin, gn_w, consts):
    b, t, _ = p_ret.shape
    c = RET_CHUNK
    rows = _seq_rows(b)
    tables = [consts[name] for name in ("ret_decay", "ret_qw", "ret_kw", "ret_cd", "blk_ret", "ret_mask")]
    return pl.pallas_call(
        _ret_body,
        grid=(b // rows, t // c),
        in_specs=[_seq_spec(c, RET_COLS, rows), _seq_spec(c, LANES, rows), _seq_spec(c, LANES, rows),
                  _seq_const((1, D_RET))] + [_seq_const(a.shape) for a in tables],
        out_specs=_seq_spec(c, D_RET, rows),
        out_shape=jax.ShapeDtypeStruct((b, t, D_RET), F32),
        scratch_shapes=[pltpu.VMEM((rows, D_RET, D_RET), F32)],
        compiler_params=_seq_params(),
        name="retention",
    )(p_ret, cos, sin, gn_w, *tables)


def _gdn_body(p_ref, misc_ref, convw_ref, alog_ref, dtb_ref, nw_ref, tril_ref, blk_ref, eg_ref, eb_ref, el_ref,
              y_ref, buf_ref, s_ref, os_ref):
    rows, c, _ = p_ref.shape

    @pl.when(pl.program_id(1) == 0)
    def _():
        s_ref[...] = jnp.zeros(s_ref.shape, F32)
        _zero_tails(buf_ref)

    gate = jnp.concatenate([p_ref[r, :, 3 * D_GDN:4 * D_GDN] for r in range(rows)], axis=0)
    taps = _shifted(buf_ref, p_ref, tuple(GDN_CONV - 1 - j for j in range(GDN_CONV)))
    conv = taps[0] * convw_ref[0:1, :]
    for j in range(1, GDN_CONV):
        conv = conv + taps[j] * convw_ref[j:j + 1, :]
    qkv = _silu(conv)
    blk = blk_ref[...]
    q = qkv[:, 0:D_GDN]
    k = qkv[:, D_GDN:2 * D_GDN]
    v = qkv[:, 2 * D_GDN:3 * D_GDN]
    q = q * lax.rsqrt(_dot_exact_lhs(q * q, blk) + L2_EPS) * HEAD_DIM ** -0.5
    k = k * lax.rsqrt(_dot_exact_lhs(k * k, blk) + L2_EPS)

    m = jnp.concatenate([misc_ref[r] for r in range(rows)], axis=0)
    beta = _sigmoid(m)
    g = -jnp.exp(alog_ref[...]) * _softplus(m + dtb_ref[...])
    gcs = _dot_exact_rhs(tril_ref[...], g)
    g_wide = _dot_exact_lhs(gcs, eg_ref[...])
    g_lane = _dot_exact_lhs(gcs, el_ref[...])
    b_lane = _dot_exact_lhs(beta, eb_ref[...])
    g_end = _last_row_of_each_chunk(g_lane, rows, c)
    eg = jnp.exp(g_lane)
    kb = k * b_lane
    left = jnp.concatenate([kb.astype(BF16), q.astype(BF16)], axis=1)
    rhs_uw = jnp.concatenate([(v * b_lane).astype(BF16), (kb * eg).astype(BF16)], axis=1)
    q_e = (q * eg).astype(BF16)
    k_tail = (k * jnp.exp(g_end - g_lane)).astype(BF16)
    s_decay = jnp.exp(g_end)
    k_b16 = k.astype(BF16)

    strict, tril, diag = _time_masks(c)
    eye = jnp.where(diag, 1.0, 0.0)
    probs = [(n, h, slice(n * c, (n + 1) * c), slice(h * PAIR, (h + 1) * PAIR))
             for n in range(rows) for h in range(H_GDN // 2)]
    qk, decay = [], []
    for n, h, rs, sl in probs:
        kq = jnp.concatenate([left[rs, sl], left[rs, D_GDN + h * PAIR:D_GDN + (h + 1) * PAIR]], axis=0)
        qk.append(_dot_nt(kq, _stack_heads(k_b16[rs, sl])))
        w0, w1 = 2 * h * LANES, (2 * h + 1) * LANES
        g_cols = jnp.concatenate([g_wide[rs, w0:w0 + LANES], g_wide[rs, w1:w1 + LANES]], axis=0)
        g_rows = g_cols.T[0:c, :]
        decay.append(jnp.where(tril, jnp.exp(jnp.where(tril, g_lane[rs, sl] - g_rows, 0.0)), 0.0))
    lmat = [jnp.where(strict, qk_[0:c] * d_, 0.0) for qk_, d_ in zip(qk, decay)]
    tm = [eye - l_ for l_ in lmat]
    pw = [l_.astype(BF16) for l_ in lmat]
    pw = [jnp.dot(p_, _stack_heads(p_), preferred_element_type=F32).astype(BF16) for p_ in pw]
    for j in range(1, NEUMANN_LEVELS):
        stepped = [_power_step(p_, t_, j == NEUMANN_LEVELS - 1) for p_, t_ in zip(pw, tm)]
        pw = [sp for sp, _ in stepped]
        tm = [st for _, st in stepped]
    uw = [_dot(t_, _stack_heads(jnp.concatenate([rhs_uw[rs, sl], rhs_uw[rs, D_GDN + h * PAIR:D_GDN + (h + 1) * PAIR]],
                                                axis=1)))
          for t_, (_, h, rs, sl) in zip(tm, probs)]
    v_new = [_stack_heads((uw_[:, 0:LANES] - _dot(uw_[:, LANES:2 * LANES], s_ref[n, h])).astype(BF16))
             for uw_, (n, h, _, _) in zip(uw, probs)]
    for i, (n, h, rs, sl) in enumerate(probs):
        attn = (qk[i][c:2 * c] * decay[i]).astype(BF16)
        os_ref[rs, sl] = jnp.dot(jnp.concatenate([q_e[rs, sl], attn], axis=1),
                                 jnp.concatenate([s_ref[n, h].astype(BF16), v_new[i]], axis=0),
                                 preferred_element_type=F32)
    for i, (n, h, rs, sl) in enumerate(probs):
        s_ref[n, h] = (s_ref[n, h] * s_decay[(n + 1) * c - 1:(n + 1) * c, sl]
                       + _dot_tn(_stack_heads(k_tail[rs, sl]), v_new[i]))
    o = os_ref[...]
    ms = _dot_exact_lhs(o * o, blk) * (1.0 / HEAD_DIM)
    out = o * lax.rsqrt(ms + NORM_EPS) * nw_ref[...] * _silu(gate)
    for r in range(rows):
        y_ref[r] = out[r * c:(r + 1) * c, :]


def _gdn(p_gdn, misc, prm, consts):
    b, t, _ = p_gdn.shape
    c = CHUNK
    rows = _seq_rows(b)
    return pl.pallas_call(
        _gdn_body,
        grid=(b // rows, t // c),
        in_specs=[_seq_spec(c, GDN_MAIN_COLS, rows), _seq_spec(c, MISC_COLS, rows),
                  _seq_const((GDN_CONV, 3 * D_GDN)),
                  _seq_const((1, MISC_COLS)), _seq_const((1, MISC_COLS)), _seq_const((1, D_GDN)),
                  ] + [_seq_const(consts[name].shape) for name in ("tril", "blk_gdn", "exp_g", "exp_b", "exp_l")],
        out_specs=_seq_spec(c, D_GDN, rows),
        out_shape=jax.ShapeDtypeStruct((b, t, D_GDN), F32),
        scratch_shapes=[pltpu.VMEM((rows, SUBLANES + c, 3 * D_GDN), F32),
                        pltpu.VMEM((rows, H_GDN // 2, PAIR, PAIR), F32),
                        pltpu.VMEM((rows * c, D_GDN), F32)],
        compiler_params=_seq_params(),
        name="gated_deltanet",
    )(p_gdn, misc, prm["conv_w"], prm["a_log"], prm["dt_bias"], prm["norm_w"], consts["tril"], consts["blk_gdn"],
      consts["exp_g"], consts["exp_b"], consts["exp_l"])


def _block_ones(n):
    i = jnp.arange(n) // HEAD_DIM
    return (i[:, None] == i[None, :]).astype(BF16)


def _constants(rows):
    c = CHUNK
    idx = jnp.arange(rows * c)
    same_chunk = (idx[:, None] // c) == (idx[None, :] // c)
    tile_k = lambda m, parts: jnp.tile(m, (parts, 1))
    consts = {"tril": jnp.tile((same_chunk & (idx[:, None] >= idx[None, :])).astype(BF16), (1, SUM_PARTS)),
              "blk_rwkv": tile_k(_block_ones(D_RWKV), NORM_PARTS), "blk_ret": tile_k(_block_ones(D_RET), NORM_PARTS),
              "blk_gdn": tile_k(_block_ones(D_GDN), NORM_PARTS), "ret_mask": _block_ones(D_RET)}
    rc = RET_CHUNK
    ridx = jnp.arange(rc)
    log_gamma = jnp.log(1.0 - 2.0 ** (-5.0 - jnp.arange(H_RET, dtype=F32)))
    rel = ridx[:, None] - ridx[None, :]
    consts["ret_decay"] = jnp.where(rel >= 0, jnp.exp(log_gamma[:, None, None] * jnp.maximum(rel, 0)), 0.0)
    lanes_gamma = jnp.repeat(log_gamma, HEAD_DIM)[None, :]
    consts["ret_qw"] = jnp.exp(lanes_gamma * (ridx[:, None] + 1))
    consts["ret_kw"] = jnp.exp(lanes_gamma * (rc - 1 - ridx[:, None]))
    consts["ret_cd"] = jnp.exp(lanes_gamma * rc)
    rows = jnp.arange(MISC_COLS)[:, None]
    wide_head = jnp.arange(H_GDN * LANES)[None, :] // LANES
    lane_head = jnp.arange(D_GDN)[None, :] // HEAD_DIM
    consts["exp_g"] = tile_k((rows == MISC_ALOG + wide_head).astype(BF16), SUM_PARTS)
    consts["exp_b"] = tile_k((rows == MISC_BETA + lane_head).astype(BF16), NORM_PARTS)
    consts["exp_l"] = tile_k((rows == MISC_ALOG + lane_head).astype(BF16), SUM_PARTS)
    return consts


def _pad_rows(w, top, total):
    return jnp.pad(w, ((top, total - top - w.shape[0]), (0, 0)))


def _pad_lanes(v, left, total):
    return jnp.pad(v, (left, total - left - v.shape[0]))[None, :]


def kernel(x, positions, norm_g, ffn_w_gate, ffn_w_up, ffn_w_down, w_in_first, w_in_rest, w_out, rwkv_mu, rwkv_w0, rwkv_w_up, rwkv_a0, rwkv_a_up, rwkv_g_up, rwkv_k_k, rwkv_k_a, rwkv_r_k, rwkv_ln_w, rwkv_ln_b, rwkv_mu_vres, rwkv_v0, rwkv_v_up, ret_gn_w, gdn_conv_w, gdn_A_log, gdn_dt_bias, gdn_norm_w):
    b, t, d = x.shape
    depth = norm_g.shape[0]
    n = b * t
    assert d == D_MODEL and n % ROW_TILE == 0 and t % RET_CHUNK == 0 and t % CHUNK == 0
    consts = _constants(_seq_rows(b))
    cos, sin = _rope_tables(positions)
    x = x.reshape(n, d)
    v_first = None
    wg, wu, wd, wo = (w.astype(BF16) for w in (ffn_w_gate, ffn_w_up, ffn_w_down, w_out))
    for l in range(depth):
        ng = norm_g[l][:, None, :]
        x = _ffn(x, ng[0], ng[1], wg, wu, wd, l, 0)

        w_in = w_in_first if l == 0 else w_in_rest[l - 1]
        gdn_small = w_in[:, RWKV_COLS + RET_COLS + GDN_MAIN_COLS:IN_COLS_FIRST]
        vres_cols = w_in[:, IN_COLS_FIRST:] if l > 0 else jnp.zeros((d, RWKV_VRES_RANK), F32)
        misc_w = jnp.concatenate([gdn_small, vres_cols], axis=1)
        misc_w = jnp.pad(misc_w, ((0, 0), (0, MISC_COLS - misc_w.shape[1])))
        w_packed = jnp.concatenate([w_in[:, :RWKV_COLS + RET_COLS + GDN_MAIN_COLS], misc_w], axis=1).astype(BF16)
        p_rwkv, p_ret, p_gdn, p_misc = _inproj(x, ng[2], w_packed)
        p_rwkv = p_rwkv.reshape(b, t, RWKV_COLS)
        p_ret = p_ret.reshape(b, t, RET_COLS)
        p_gdn = p_gdn.reshape(b, t, GDN_MAIN_COLS)
        p_misc = p_misc.reshape(b, t, MISC_COLS)

        rw = {"mu": rwkv_mu[l][None, :], "w0": rwkv_w0[l][None, :],
              "w_up": _pad_rows(rwkv_w_up[l], 0, LANES).astype(BF16), "a0": rwkv_a0[l][None, :],
              "a_up": _pad_rows(rwkv_a_up[l], RWKV_DECAY_RANK, LANES).astype(BF16),
              "g_up": rwkv_g_up[l].astype(BF16), "k_k": rwkv_k_k[l][None, :], "k_a": rwkv_k_a[l][None, :],
              "r_k": rwkv_r_k[l][None, :], "ln_w": rwkv_ln_w[l][None, :], "ln_b": rwkv_ln_b[l][None, :]}
        if l > 0:
            rw["mu_v"] = _pad_lanes(rwkv_mu_vres[l - 1], MISC_VRES, MISC_COLS)
            rw["v0"] = rwkv_v0[l - 1][None, :]
            rw["v_up"] = _pad_rows(rwkv_v_up[l - 1], MISC_VRES, MISC_COLS).astype(BF16)
        y_a, v_first = _rwkv(p_rwkv, p_misc, v_first, rw, consts)
        y_b = _ret(p_ret, cos, sin, ret_gn_w[l][None, :], consts)
        gd = {"conv_w": gdn_conv_w[l], "a_log": _pad_lanes(gdn_A_log[l], MISC_ALOG, MISC_COLS),
              "dt_bias": _pad_lanes(gdn_dt_bias[l], MISC_ALOG, MISC_COLS),
              "norm_w": jnp.tile(gdn_norm_w[l], H_GDN)[None, :]}
        y_c = _gdn(p_gdn, p_misc, gd, consts)

        mixer = (y_a.reshape(n, D_RWKV), y_b.reshape(n, D_RET), y_c.reshape(n, D_GDN), wo, ng[3])
        x = _ffn(x, ng[4], ng[5], wg, wu, wd, l, 1, mixer)
    return x.reshape(b, t, d)
```

```python
import functools
import math

import jax
import jax.numpy as jnp
from jax import lax
from jax.experimental import pallas as pl
from jax.experimental.pallas import tpu as pltpu

F32 = jnp.float32
BF16 = jnp.bfloat16

D_MODEL = 1024
D_FF = 2816
HEAD_DIM = 64
H_RWKV, H_RET, H_GDN = 6, 4, 6
D_RWKV, D_RET, D_GDN = H_RWKV * HEAD_DIM, H_RET * HEAD_DIM, H_GDN * HEAD_DIM
RWKV_DECAY_RANK, RWKV_ICLR_RANK, RWKV_VRES_RANK, RWKV_GATE_RANK = 64, 64, 32, 128
RWKV_COLS = 3 * D_RWKV + RWKV_DECAY_RANK + RWKV_ICLR_RANK + RWKV_GATE_RANK
RET_COLS = 4 * D_RET
GDN_MAIN_COLS = 4 * D_GDN
GDN_COLS = GDN_MAIN_COLS + 2 * H_GDN
IN_COLS_FIRST = RWKV_COLS + RET_COLS + GDN_COLS
GDN_CONV = 4
NORM_EPS = 1e-6
L2_EPS = 1e-6
RWKV_GN_EPS = HEAD_DIM * 1e-5
RET_GN_EPS = 1e-5
ROPE_THETA = 10000.0

LANES = 128
SUBLANES = 8
MISC_COLS = LANES
MISC_BETA, MISC_ALOG, MISC_VRES = 0, H_GDN, 2 * H_GDN
CHUNK = 64
RET_CHUNK = 128
PAIR = 2 * HEAD_DIM
SEQ_ROWS = 8
ROW_TILE = 512
FFN_ROW_TILE = 1024
FF_CHUNK = 512
SUM_PARTS = 3
NORM_PARTS = 1
VMEM_LIMIT = 56 * 1024 * 1024


def _dot(a, b):
    return jnp.dot(a.astype(BF16), b.astype(BF16), preferred_element_type=F32)


def _dot_nt(a, b):
    return lax.dot_general(a.astype(BF16), b.astype(BF16), (((1,), (1,)), ((), ())),
                           preferred_element_type=F32)


def _dot_tn(a, b):
    return lax.dot_general(a.astype(BF16), b.astype(BF16), (((0,), (0,)), ((), ())),
                           preferred_element_type=F32)


def _split(x, parts):
    out = []
    for _ in range(parts - 1):
        h = x.astype(BF16)
        out.append(h)
        x = x - h.astype(F32)
    out.append(x.astype(BF16))
    return out


def _dot_exact_rhs(a_tiled, x):
    parts = a_tiled.shape[1] // x.shape[0]
    return jnp.dot(a_tiled, jnp.concatenate(_split(x, parts), axis=0), preferred_element_type=F32)


def _dot_exact_lhs(x, b_tiled):
    parts = b_tiled.shape[0] // x.shape[1]
    return jnp.dot(jnp.concatenate(_split(x, parts), axis=1), b_tiled, preferred_element_type=F32)


def _head_sums(x, blk_pair):
    return jnp.concatenate([_dot_exact_lhs(x[:, i:i + PAIR], blk_pair) for i in range(0, x.shape[1], PAIR)], axis=1)


def _rms(x, g):
    return x * lax.rsqrt(jnp.mean(x * x, axis=-1, keepdims=True) + NORM_EPS) * g


def _sigmoid(x):
    return 1.0 / (1.0 + jnp.exp(-x))


def _silu(x):
    return x * _sigmoid(x)


def _softplus(x):
    return jnp.maximum(x, 0.0) + jnp.log(1.0 + jnp.exp(-jnp.abs(x)))


def _stack_heads(x):
    lane = lax.broadcasted_iota(jnp.int32, x.shape, 1) % PAIR
    return jnp.concatenate([jnp.where(lane < HEAD_DIM, x, 0.0), jnp.where(lane >= HEAD_DIM, x, 0.0)], axis=0)


def _time_masks(c):
    i = lax.broadcasted_iota(jnp.int32, (c, 2 * c), 0)
    j = lax.broadcasted_iota(jnp.int32, (c, 2 * c), 1) % c
    return i > j, i >= j, i == j


assert CHUNK == HEAD_DIM
NEUMANN_LEVELS = int(math.log2(CHUNK))


def _power_step(p, x, last):
    n = p.shape[1]
    if last:
        return None, x + jnp.dot(p, _stack_heads(x.astype(BF16)), preferred_element_type=F32)
    z = jnp.dot(p, _stack_heads(jnp.concatenate([p, x.astype(BF16)], axis=1)), preferred_element_type=F32)
    return z[:, :n].astype(BF16), x + z[:, n:]


def _ffn_body(after_mixer, *refs):
    if after_mixer:
        (x_ref, ya_ref, yb_ref, yc_ref, wo_ref, gmix_ref,
         gpre_ref, gpost_ref, wg_ref, wu_ref, wd_ref, o_ref, h_ref, a_ref) = refs
        y_all = jnp.concatenate([r[...].astype(BF16) for r in (ya_ref, yb_ref, yc_ref)], axis=1)
        x = x_ref[...] + _rms(jnp.dot(y_all, wo_ref[...], preferred_element_type=F32), gmix_ref[...])
    else:
        x_ref, gpre_ref, gpost_ref, wg_ref, wu_ref, wd_ref, o_ref, h_ref, a_ref = refs
        x = x_ref[...]
    h_ref[...] = _rms(x, gpre_ref[...]).astype(BF16)
    for c0 in range(0, D_FF, FF_CHUNK):
        w = min(FF_CHUNK, D_FF - c0)
        h = h_ref[...]
        g = jnp.dot(h, wg_ref[:, c0:c0 + w], preferred_element_type=F32)
        u = jnp.dot(h, wu_ref[:, c0:c0 + w], preferred_element_type=F32)
        a_ref[:, c0:c0 + w] = (_silu(g) * u).astype(BF16)
    y = jnp.dot(a_ref[...], wd_ref[...], preferred_element_type=F32)
    o_ref[...] = x + 0.5 * _rms(y, gpost_ref[...])


def _row_spec(width, tile=ROW_TILE):
    return pl.BlockSpec((tile, width), lambda i: (i, 0))


def _const_spec(shape):
    zeros = (0,) * len(shape)
    return pl.BlockSpec(shape, lambda *_: zeros, pipeline_mode=pl.Buffered(1))


def _dense_params():
    return pltpu.CompilerParams(dimension_semantics=("parallel",), vmem_limit_bytes=VMEM_LIMIT)


def _layer_spec(shape, *lead):
    return pl.BlockSpec((None,) * len(lead) + shape, lambda *_: lead + (0,) * len(shape),
                        pipeline_mode=pl.Buffered(1))


def _ffn(x, g_pre, g_post, wg, wu, wd, layer, half, mixer=None):
    n = x.shape[0]
    tile = math.gcd(n, FFN_ROW_TILE)
    ins, in_specs = [x], [_row_spec(D_MODEL, tile)]
    if mixer is not None:
        ya, yb, yc, wo, g_mix = mixer
        ins += [ya, yb, yc, wo, g_mix]
        in_specs += [_row_spec(D_RWKV, tile), _row_spec(D_RET, tile), _row_spec(D_GDN, tile),
                     _layer_spec((D_RWKV + D_RET + D_GDN, D_MODEL), layer), _const_spec((1, D_MODEL))]
    ins += [g_pre, g_post, wg, wu, wd]
    in_specs += [_const_spec((1, D_MODEL)), _const_spec((1, D_MODEL)),
                 _layer_spec((D_MODEL, D_FF), layer, half), _layer_spec((D_MODEL, D_FF), layer, half),
                 _layer_spec((D_FF, D_MODEL), layer, half)]
    return pl.pallas_call(
        functools.partial(_ffn_body, mixer is not None),
        grid=(n // tile,),
        in_specs=in_specs,
        out_specs=_row_spec(D_MODEL, tile),
        out_shape=jax.ShapeDtypeStruct((n, D_MODEL), F32),
        scratch_shapes=[pltpu.VMEM((tile, D_MODEL), BF16), pltpu.VMEM((tile, D_FF), BF16)],
        compiler_params=_dense_params(),
        name="ffn_after_mixer" if mixer is not None else "ffn",
    )(*ins)


_IN_WIDTHS = (RWKV_COLS, RET_COLS, GDN_MAIN_COLS, MISC_COLS)
IN_COLS_PACKED = sum(_IN_WIDTHS)


def _inproj_body(x_ref, g_ref, w_ref, o_rwkv, o_ret, o_gdn, o_misc, h_ref):
    h_ref[...] = _rms(x_ref[...], g_ref[...]).astype(BF16)
    off = 0
    for o_ref in (o_rwkv, o_ret, o_gdn, o_misc):
        width = o_ref.shape[-1]
        for c0 in range(0, width, FF_CHUNK):
            w = min(FF_CHUNK, width - c0)
            o_ref[:, c0:c0 + w] = jnp.dot(h_ref[...], w_ref[:, off + c0:off + c0 + w],
                                          preferred_element_type=F32)
        off += width


def _inproj(x, g, w):
    n = x.shape[0]
    return pl.pallas_call(
        _inproj_body,
        grid=(n // ROW_TILE,),
        in_specs=[_row_spec(D_MODEL), _const_spec((1, D_MODEL)), _const_spec((D_MODEL, IN_COLS_PACKED))],
        out_specs=[_row_spec(w_) for w_ in _IN_WIDTHS],
        out_shape=[jax.ShapeDtypeStruct((n, w_), F32) for w_ in _IN_WIDTHS],
        scratch_shapes=[pltpu.VMEM((ROW_TILE, D_MODEL), BF16)],
        compiler_params=_dense_params(),
        name="inproj",
    )(x, g, w)


def _rope_body(pos_ref, invf_ref, sign_ref, cos_ref, sin_ref):
    ang = pos_ref[0] * invf_ref[...]
    cos_ref[0] = jnp.cos(ang)
    sin_ref[0] = jnp.sin(ang) * sign_ref[...]


def _rope_tables(positions):
    b, t = positions.shape
    half = HEAD_DIM // 2
    inv_freq = ROPE_THETA ** (-jnp.arange(0, HEAD_DIM, 2, dtype=F32) / HEAD_DIM)
    invf = jnp.tile(inv_freq, LANES // half)[None, :]
    sign = jnp.tile(jnp.concatenate([-jnp.ones((half,), F32), jnp.ones((half,), F32)]), LANES // HEAD_DIM)[None, :]
    pos = positions.astype(F32)[..., None]
    tt = min(t, 512)
    return pl.pallas_call(
        _rope_body,
        grid=(b, t // tt),
        in_specs=[pl.BlockSpec((1, tt, 1), lambda i, j: (i, j, 0)),
                  pl.BlockSpec((1, LANES), lambda i, j: (0, 0)),
                  pl.BlockSpec((1, LANES), lambda i, j: (0, 0))],
        out_specs=[pl.BlockSpec((1, tt, LANES), lambda i, j: (i, j, 0))] * 2,
        out_shape=[jax.ShapeDtypeStruct((b, t, LANES), F32)] * 2,
        compiler_params=pltpu.CompilerParams(dimension_semantics=("parallel", "parallel")),
        name="rope_tables",
    )(pos, invf, sign)


def _seq_spec(c, width, rows=1):
    return pl.BlockSpec((rows, c, width), lambda b, t: (b, t, 0))


def _seq_const(shape):
    zeros = (0,) * len(shape)
    return pl.BlockSpec(shape, lambda b, t: zeros)


def _seq_rows(b):
    return math.gcd(b, SEQ_ROWS)


def _seq_params():
    return pltpu.CompilerParams(dimension_semantics=("parallel", "arbitrary"), vmem_limit_bytes=VMEM_LIMIT)


def _shifted(buf_ref, x_ref, shifts):
    rows, c, _ = x_ref.shape
    width = buf_ref.shape[-1]
    out = [[] for _ in shifts]
    for r in range(rows):
        x = x_ref[r, :, 0:width]
        buf_ref[r, SUBLANES:SUBLANES + c, :] = x
        for i, s in enumerate(shifts):
            out[i].append(buf_ref[r, SUBLANES - s:SUBLANES - s + c, :])
        buf_ref[r, 0:SUBLANES, :] = x[c - SUBLANES:c, :]
    return [jnp.concatenate(o, axis=0) for o in out]


def _zero_tails(buf_ref):
    rows, _, width = buf_ref.shape
    buf_ref[:, 0:SUBLANES, :] = jnp.zeros((rows, SUBLANES, width), F32)


def _last_row_of_each_chunk(x, rows, c):
    return jnp.concatenate([jnp.broadcast_to(x[(r + 1) * c - 1:(r + 1) * c, :], (c, x.shape[1]))
                            for r in range(rows)], axis=0)


def _rwkv_body(has_vres, *refs):
    if has_vres:
        (p_ref, misc_ref, vf_ref, mu_ref, w0_ref, wup_ref, a0_ref, aup_ref, gup_ref, kk_ref, ka_ref, rk_ref,
         lnw_ref, lnb_ref, muv_ref, v0_ref, vup_ref, tril_ref, blk_ref,
         y_ref, buf_ref, bufv_ref, s_ref, ys_ref) = refs
    else:
        (p_ref, mu_ref, w0_ref, wup_ref, a0_ref, aup_ref, gup_ref, kk_ref, ka_ref, rk_ref,
         lnw_ref, lnb_ref, tril_ref, blk_ref,
         y_ref, vf_out_ref, buf_ref, s_ref, ys_ref) = refs
        bufv_ref = None
    rows, c, _ = p_ref.shape

    @pl.when(pl.program_id(1) == 0)
    def _():
        s_ref[...] = jnp.zeros(s_ref.shape, F32)
        _zero_tails(buf_ref)
        if bufv_ref is not None:
            _zero_tails(bufv_ref)

    p = jnp.concatenate([p_ref[r] for r in range(rows)], axis=0)
    (prev,) = _shifted(buf_ref, p_ref, (1,))
    ps = p + (prev - p) * mu_ref[...]
    rec = ps[:, 0:D_RWKV]
    k = ps[:, D_RWKV:2 * D_RWKV]
    v = ps[:, 2 * D_RWKV:3 * D_RWKV]
    wa = ps[:, 3 * D_RWKV:3 * D_RWKV + LANES]
    gc = ps[:, 3 * D_RWKV + LANES:RWKV_COLS]
    logw = -math.exp(-0.5) * _sigmoid(w0_ref[...] + _dot(jnp.tanh(wa), wup_ref[...]))
    a = _sigmoid(a0_ref[...] + _dot(wa, aup_ref[...]))
    g = _dot(_sigmoid(gc), gup_ref[...])
    if has_vres:
        pm = jnp.concatenate([misc_ref[r] for r in range(rows)], axis=0)
        (prevm,) = _shifted(bufv_ref, misc_ref, (1,))
        pmv = pm + (prevm - pm) * muv_ref[...]
        vf = jnp.concatenate([vf_ref[r] for r in range(rows)], axis=0)
        v = v + (vf - v) * _sigmoid(v0_ref[...] + _dot(pmv, vup_ref[...]))
    else:
        for r in range(rows):
            vf_out_ref[r] = v[r * c:(r + 1) * c, :]

    blk = blk_ref[...]
    kkr = k * kk_ref[...]
    kk = kkr * lax.rsqrt(_head_sums(kkr * kkr, blk) + L2_EPS)
    k2 = k * (1.0 + (a - 1.0) * ka_ref[...])

    cw = _dot_exact_rhs(tril_ref[...], logw)
    cw_last = _last_row_of_each_chunk(cw, rows, c)
    dec_in = jnp.exp(cw)
    dec_ex = jnp.exp(cw - logw)
    dec_inv = jnp.exp(-cw)
    dec_tail = jnp.exp(cw_last - cw)
    dec_all = jnp.exp(cw_last)
    kka = kk * a
    b_p = (kk * dec_ex).astype(BF16)
    r_p = (rec * dec_in).astype(BF16)
    a_i = (-kka * dec_inv).astype(BF16)
    k_i = (k2 * dec_inv).astype(BF16)
    a_d = (-kka * dec_tail).astype(BF16)
    k_d = (k2 * dec_tail).astype(BF16)
    v_b = v.astype(BF16)

    strict, tril, _ = _time_masks(c)
    tril2 = jnp.concatenate([tril, tril], axis=1)
    probs = [(n, h, slice(n * c, (n + 1) * c), slice(h * PAIR, (h + 1) * PAIR))
             for n in range(rows) for h in range(H_RWKV // 2)]
    left = [jnp.concatenate([b_p[rs, sl], r_p[rs, sl]], axis=0) for _, _, rs, sl in probs]
    v_st = [_stack_heads(v_b[rs, sl]) for _, _, rs, sl in probs]
    m = [_dot_nt(lf, jnp.concatenate([_stack_heads(a_i[rs, sl]), _stack_heads(k_i[rs, sl])], axis=0))
         for lf, (_, _, rs, sl) in zip(left, probs)]
    x0 = [_dot_nt(lf, s_ref[n, h]) for lf, (n, h, _, _) in zip(left, probs)]
    u = [xi[0:c] + _dot(jnp.where(strict, mi[0:c, 2 * c:4 * c], 0.0), vi) for xi, mi, vi in zip(x0, m, v_st)]
    pw = [jnp.where(strict, mi[0:c, 0:2 * c], 0.0).astype(BF16) for mi in m]
    for j in range(NEUMANN_LEVELS):
        stepped = [_power_step(pi, ui, j == NEUMANN_LEVELS - 1) for pi, ui in zip(pw, u)]
        pw = [sp for sp, _ in stepped]
        u = [su for _, su in stepped]
    uv = [jnp.concatenate([_stack_heads(ui.astype(BF16)), vi], axis=0) for ui, vi in zip(u, v_st)]
    for i, (n, h, rs, sl) in enumerate(probs):
        ys_ref[rs, sl] = x0[i][c:2 * c] + _dot(jnp.where(tril2, m[i][c:2 * c, :], 0.0), uv[i])
    for i, (n, h, rs, sl) in enumerate(probs):
        tails = jnp.concatenate([_stack_heads(a_d[rs, sl]), _stack_heads(k_d[rs, sl])], axis=0)
        s_ref[n, h] = s_ref[n, h] * dec_all[(n + 1) * c - 1:(n + 1) * c, sl] + _dot_tn(uv[i], tails)
    y = ys_ref[...]

    inv_d = 1.0 / HEAD_DIM
    mean = _head_sums(y, blk) * inv_d
    yc = y - mean
    var = _head_sums(yc * yc, blk) * inv_d
    yn = yc * lax.rsqrt(var + RWKV_GN_EPS) * lnw_ref[...] + lnb_ref[...]
    bonus = _head_sums(rec * k2 * rk_ref[...], blk) * v
    out = (yn + bonus) * g
    for r in range(rows):
        y_ref[r] = out[r * c:(r + 1) * c, :]


def _rwkv(p_rwkv, misc, v_first, prm, consts):
    b, t, _ = p_rwkv.shape
    c = CHUNK
    has_vres = v_first is not None
    row = lambda n: _seq_const((1, n))
    common = [prm["mu"], prm["w0"], prm["w_up"], prm["a0"], prm["a_up"], prm["g_up"], prm["k_k"], prm["k_a"],
              prm["r_k"], prm["ln_w"], prm["ln_b"]]
    common_specs = [row(RWKV_COLS), row(D_RWKV), _seq_const((LANES, D_RWKV)), row(D_RWKV),
                    _seq_const((LANES, D_RWKV)), _seq_const((RWKV_GATE_RANK, D_RWKV)), row(D_RWKV), row(D_RWKV),
                    row(D_RWKV), row(D_RWKV), row(D_RWKV)]
    rows = _seq_rows(b)
    const_in = [consts["tril"], consts["blk_pair"]]
    const_specs = [_seq_const(a.shape) for a in const_in]
    state = pltpu.VMEM((rows, H_RWKV // 2, PAIR, PAIR), F32)
    ys = pltpu.VMEM((rows * c, D_RWKV), F32)
    if has_vres:
        ins = [p_rwkv, misc, v_first] + common + [prm["mu_v"], prm["v0"], prm["v_up"]] + const_in
        in_specs = ([_seq_spec(c, RWKV_COLS, rows), _seq_spec(c, MISC_COLS, rows), _seq_spec(c, D_RWKV, rows)]
                    + common_specs + [row(MISC_COLS), row(D_RWKV), _seq_const((MISC_COLS, D_RWKV))] + const_specs)
        out_specs = _seq_spec(c, D_RWKV, rows)
        out_shape = jax.ShapeDtypeStruct((b, t, D_RWKV), F32)
        scratch = [pltpu.VMEM((rows, SUBLANES + c, RWKV_COLS), F32),
                   pltpu.VMEM((rows, SUBLANES + c, MISC_COLS), F32), state, ys]
    else:
        ins = [p_rwkv] + common + const_in
        in_specs = [_seq_spec(c, RWKV_COLS, rows)] + common_specs + const_specs
        out_specs = [_seq_spec(c, D_RWKV, rows)] * 2
        out_shape = [jax.ShapeDtypeStruct((b, t, D_RWKV), F32)] * 2
        scratch = [pltpu.VMEM((rows, SUBLANES + c, RWKV_COLS), F32), state, ys]
    out = pl.pallas_call(
        functools.partial(_rwkv_body, has_vres),
        grid=(b // rows, t // c),
        in_specs=in_specs, out_specs=out_specs, out_shape=out_shape, scratch_shapes=scratch,
        compiler_params=_seq_params(),
        name="rwkv7_vres" if has_vres else "rwkv7_first",
    )(*ins)
    return (out, v_first) if has_vres else (out[0], out[1])


def _ret_body(p_ref, cos_ref, sin_ref, gnw_ref, dec_ref, qw_ref, kw_ref, cd_ref, blk_ref, mask_ref, y_ref, r_ref):
    rows, c, _ = p_ref.shape

    @pl.when(pl.program_id(1) == 0)
    def _():
        r_ref[...] = jnp.zeros(r_ref.shape, F32)

    stack = lambda ref, reps: jnp.concatenate(
        [jnp.concatenate([ref[n]] * reps, axis=1) for n in range(rows)], axis=0)
    p = stack(p_ref, 1)
    q = p[:, 0:D_RET]
    k = p[:, D_RET:2 * D_RET]
    v = p[:, 2 * D_RET:3 * D_RET].astype(BF16)
    gate = p[:, 3 * D_RET:4 * D_RET]
    cos = stack(cos_ref, D_RET // LANES)
    sin = stack(sin_ref, D_RET // LANES)
    lane = lax.broadcasted_iota(jnp.int32, q.shape, 1)
    first_half = (lane % HEAD_DIM) < (HEAD_DIM // 2)
    half = HEAD_DIM // 2

    def rope(x):
        partner = jnp.where(first_half, pltpu.roll(x, D_RET - half, 1), pltpu.roll(x, half, 1))
        return x * cos + partner * sin

    q = rope(q) * HEAD_DIM ** -0.5
    k = rope(k)
    q_in = (q * jnp.concatenate([qw_ref[...]] * rows, axis=0)).astype(BF16)
    k_out = (k * jnp.concatenate([kw_ref[...]] * rows, axis=0)).astype(BF16)
    q = q.astype(BF16)
    k = k.astype(BF16)
    chunk_lane = lax.broadcasted_iota(jnp.int32, (c, D_RET), 1)
    in_head = [(chunk_lane // HEAD_DIM) == h for h in range(H_RET)]
    chunks = [(n, slice(n * c, (n + 1) * c)) for n in range(rows)]
    scores = [[(_dot_nt(jnp.where(m, q[rs], 0.0), k[rs]) * dec_ref[h]).astype(BF16) for h, m in enumerate(in_head)]
              for _, rs in chunks]
    outs = []
    for (n, rs), sc in zip(chunks, scores):
        lhs = jnp.concatenate(sc + [q_in[rs]], axis=1)
        rhs = jnp.concatenate([jnp.where(m, v[rs], 0.0) for m in in_head] + [r_ref[n].astype(BF16)], axis=0)
        outs.append(jnp.dot(lhs, rhs, preferred_element_type=F32))
    for n, rs in chunks:
        r_ref[n] = r_ref[n] * cd_ref[...] + _dot_tn(k_out[rs], v[rs]) * mask_ref[...].astype(F32)
    o = jnp.concatenate(outs, axis=0)

    blk = blk_ref[...]
    inv_d = 1.0 / HEAD_DIM
    mean = _head_sums(o, blk) * inv_d
    oc = o - mean
    var = _head_sums(oc * oc, blk) * inv_d
    out = _silu(gate) * (oc * lax.rsqrt(var + RET_GN_EPS) * gnw_ref[...])
    for n, rs in chunks:
        y_ref[n] = out[rs]


def _ret(p_ret, cos, sin, gn_w, consts):
    b, t, _ = p_ret.shape
    c = RET_CHUNK
    rows = _seq_rows(b)
    tables = [consts[name] for name in ("ret_decay", "ret_qw", "ret_kw", "ret_cd", "blk_pair", "ret_mask")]
    return pl.pallas_call(
        _ret_body,
        grid=(b // rows, t // c),
        in_specs=[_seq_spec(c, RET_COLS, rows), _seq_spec(c, LANES, rows), _seq_spec(c, LANES, rows),
                  _seq_const((1, D_RET))] + [_seq_const(a.shape) for a in tables],
        out_specs=_seq_spec(c, D_RET, rows),
        out_shape=jax.ShapeDtypeStruct((b, t, D_RET), F32),
        scratch_shapes=[pltpu.VMEM((rows, D_RET, D_RET), F32)],
        compiler_params=_seq_params(),
        name="retention",
    )(p_ret, cos, sin, gn_w, *tables)


def _gdn_body(p_ref, misc_ref, convw_ref, alog_ref, dtb_ref, nw_ref, tril_ref, blk_ref, eg_ref, eb_ref, el_ref,
              y_ref, buf_ref, s_ref, os_ref):
    rows, c, _ = p_ref.shape

    @pl.when(pl.program_id(1) == 0)
    def _():
        s_ref[...] = jnp.zeros(s_ref.shape, F32)
        _zero_tails(buf_ref)

    gate = jnp.concatenate([p_ref[r, :, 3 * D_GDN:4 * D_GDN] for r in range(rows)], axis=0)
    taps = _shifted(buf_ref, p_ref, tuple(GDN_CONV - 1 - j for j in range(GDN_CONV)))
    conv = taps[0] * convw_ref[0:1, :]
    for j in range(1, GDN_CONV):
        conv = conv + taps[j] * convw_ref[j:j + 1, :]
    qkv = _silu(conv)
    blk = blk_ref[...]
    q = qkv[:, 0:D_GDN]
    k = qkv[:, D_GDN:2 * D_GDN]
    v = qkv[:, 2 * D_GDN:3 * D_GDN]
    q = q * lax.rsqrt(_head_sums(q * q, blk) + L2_EPS) * HEAD_DIM ** -0.5
    k = k * lax.rsqrt(_head_sums(k * k, blk) + L2_EPS)

    m = jnp.concatenate([misc_ref[r] for r in range(rows)], axis=0)
    beta = _sigmoid(m)
    g = -jnp.exp(alog_ref[...]) * _softplus(m + dtb_ref[...])
    gcs = _dot_exact_rhs(tril_ref[...], g)
    g_wide = _dot_exact_lhs(gcs, eg_ref[...])
    g_lane = _dot_exact_lhs(gcs, el_ref[...])
    b_lane = _dot_exact_lhs(beta, eb_ref[...])
    g_end = _last_row_of_each_chunk(g_lane, rows, c)
    eg = jnp.exp(g_lane)
    kb = k * b_lane
    left = jnp.concatenate([kb.astype(BF16), q.astype(BF16)], axis=1)
    rhs_uw = jnp.concatenate([(v * b_lane).astype(BF16), (kb * eg).astype(BF16)], axis=1)
    q_e = (q * eg).astype(BF16)
    k_tail = (k * jnp.exp(g_end - g_lane)).astype(BF16)
    s_decay = jnp.exp(g_end)
    k_b16 = k.astype(BF16)

    strict, tril, diag = _time_masks(c)
    eye = jnp.where(diag, 1.0, 0.0)
    probs = [(n, h, slice(n * c, (n + 1) * c), slice(h * PAIR, (h + 1) * PAIR))
             for n in range(rows) for h in range(H_GDN // 2)]
    qk, decay = [], []
    for n, h, rs, sl in probs:
        kq = jnp.concatenate([left[rs, sl], left[rs, D_GDN + h * PAIR:D_GDN + (h + 1) * PAIR]], axis=0)
        qk.append(_dot_nt(kq, _stack_heads(k_b16[rs, sl])))
        w0, w1 = 2 * h * LANES, (2 * h + 1) * LANES
        g_cols = jnp.concatenate([g_wide[rs, w0:w0 + LANES], g_wide[rs, w1:w1 + LANES]], axis=0)
        g_rows = g_cols.T[0:c, :]
        decay.append(jnp.where(tril, jnp.exp(jnp.where(tril, g_lane[rs, sl] - g_rows, 0.0)), 0.0))
    lmat = [jnp.where(strict, qk_[0:c] * d_, 0.0) for qk_, d_ in zip(qk, decay)]
    tm = [eye - l_ for l_ in lmat]
    pw = [l_.astype(BF16) for l_ in lmat]
    pw = [jnp.dot(p_, _stack_heads(p_), preferred_element_type=F32).astype(BF16) for p_ in pw]
    for j in range(1, NEUMANN_LEVELS):
        stepped = [_power_step(p_, t_, j == NEUMANN_LEVELS - 1) for p_, t_ in zip(pw, tm)]
        pw = [sp for sp, _ in stepped]
        tm = [st for _, st in stepped]
    uw = [_dot(t_, _stack_heads(jnp.concatenate([rhs_uw[rs, sl], rhs_uw[rs, D_GDN + h * PAIR:D_GDN + (h + 1) * PAIR]],
                                                axis=1)))
          for t_, (_, h, rs, sl) in zip(tm, probs)]
    v_new = [_stack_heads((uw_[:, 0:LANES] - _dot(uw_[:, LANES:2 * LANES], s_ref[n, h])).astype(BF16))
             for uw_, (n, h, _, _) in zip(uw, probs)]
    for i, (n, h, rs, sl) in enumerate(probs):
        attn = (qk[i][c:2 * c] * decay[i]).astype(BF16)
        os_ref[rs, sl] = jnp.dot(jnp.concatenate([q_e[rs, sl], attn], axis=1),
                                 jnp.concatenate([s_ref[n, h].astype(BF16), v_new[i]], axis=0),
                                 preferred_element_type=F32)
    for i, (n, h, rs, sl) in enumerate(probs):
        s_ref[n, h] = (s_ref[n, h] * s_decay[(n + 1) * c - 1:(n + 1) * c, sl]
                       + _dot_tn(_stack_heads(k_tail[rs, sl]), v_new[i]))
    o = os_ref[...]
    ms = _head_sums(o * o, blk) * (1.0 / HEAD_DIM)
    out = o * lax.rsqrt(ms + NORM_EPS) * nw_ref[...] * _silu(gate)
    for r in range(rows):
        y_ref[r] = out[r * c:(r + 1) * c, :]


def _gdn(p_gdn, misc, prm, consts):
    b, t, _ = p_gdn.shape
    c = CHUNK
    rows = _seq_rows(b)
    return pl.pallas_call(
        _gdn_body,
        grid=(b // rows, t // c),
        in_specs=[_seq_spec(c, GDN_MAIN_COLS, rows), _seq_spec(c, MISC_COLS, rows),
                  _seq_const((GDN_CONV, 3 * D_GDN)),
                  _seq_const((1, MISC_COLS)), _seq_const((1, MISC_COLS)), _seq_const((1, D_GDN)),
                  ] + [_seq_const(consts[name].shape) for name in ("tril", "blk_pair", "exp_g", "exp_b", "exp_l")],
        out_specs=_seq_spec(c, D_GDN, rows),
        out_shape=jax.ShapeDtypeStruct((b, t, D_GDN), F32),
        scratch_shapes=[pltpu.VMEM((rows, SUBLANES + c, 3 * D_GDN), F32),
                        pltpu.VMEM((rows, H_GDN // 2, PAIR, PAIR), F32),
                        pltpu.VMEM((rows * c, D_GDN), F32)],
        compiler_params=_seq_params(),
        name="gated_deltanet",
    )(p_gdn, misc, prm["conv_w"], prm["a_log"], prm["dt_bias"], prm["norm_w"], consts["tril"], consts["blk_pair"],
      consts["exp_g"], consts["exp_b"], consts["exp_l"])


def _block_ones(n):
    i = jnp.arange(n) // HEAD_DIM
    return (i[:, None] == i[None, :]).astype(BF16)


def _constants(rows):
    c = CHUNK
    idx = jnp.arange(rows * c)
    same_chunk = (idx[:, None] // c) == (idx[None, :] // c)
    tile_k = lambda m, parts: jnp.tile(m, (parts, 1))
    consts = {"tril": jnp.tile((same_chunk & (idx[:, None] >= idx[None, :])).astype(BF16), (1, SUM_PARTS)),
              "blk_pair": tile_k(_block_ones(PAIR), NORM_PARTS), "ret_mask": _block_ones(D_RET)}
    rc = RET_CHUNK
    ridx = jnp.arange(rc)
    log_gamma = jnp.log(1.0 - 2.0 ** (-5.0 - jnp.arange(H_RET, dtype=F32)))
    rel = ridx[:, None] - ridx[None, :]
    consts["ret_decay"] = jnp.where(rel >= 0, jnp.exp(log_gamma[:, None, None] * jnp.maximum(rel, 0)), 0.0)
    lanes_gamma = jnp.repeat(log_gamma, HEAD_DIM)[None, :]
    consts["ret_qw"] = jnp.exp(lanes_gamma * (ridx[:, None] + 1))
    consts["ret_kw"] = jnp.exp(lanes_gamma * (rc - 1 - ridx[:, None]))
    consts["ret_cd"] = jnp.exp(lanes_gamma * rc)
    rows = jnp.arange(MISC_COLS)[:, None]
    wide_head = jnp.arange(H_GDN * LANES)[None, :] // LANES
    lane_head = jnp.arange(D_GDN)[None, :] // HEAD_DIM
    consts["exp_g"] = tile_k((rows == MISC_ALOG + wide_head).astype(BF16), SUM_PARTS)
    consts["exp_b"] = tile_k((rows == MISC_BETA + lane_head).astype(BF16), NORM_PARTS)
    consts["exp_l"] = tile_k((rows == MISC_ALOG + lane_head).astype(BF16), SUM_PARTS)
    return consts


def _pad_rows(w, top, total):
    return jnp.pad(w, ((top, total - top - w.shape[0]), (0, 0)))


def _pad_lanes(v, left, total):
    return jnp.pad(v, (left, total - left - v.shape[0]))[None, :]


def kernel(x, positions, norm_g, ffn_w_gate, ffn_w_up, ffn_w_down, w_in_first, w_in_rest, w_out, rwkv_mu, rwkv_w0, rwkv_w_up, rwkv_a0, rwkv_a_up, rwkv_g_up, rwkv_k_k, rwkv_k_a, rwkv_r_k, rwkv_ln_w, rwkv_ln_b, rwkv_mu_vres, rwkv_v0, rwkv_v_up, ret_gn_w, gdn_conv_w, gdn_A_log, gdn_dt_bias, gdn_norm_w):
    b, t, d = x.shape
    depth = norm_g.shape[0]
    n = b * t
    assert d == D_MODEL and n % ROW_TILE == 0 and t % RET_CHUNK == 0 and t % CHUNK == 0
    consts = _constants(_seq_rows(b))
    cos, sin = _rope_tables(positions)
    x = x.reshape(n, d)
    v_first = None
    wg, wu, wd, wo = (w.astype(BF16) for w in (ffn_w_gate, ffn_w_up, ffn_w_down, w_out))
    for l in range(depth):
        ng = norm_g[l][:, None, :]
        x = _ffn(x, ng[0], ng[1], wg, wu, wd, l, 0)

        w_in = w_in_first if l == 0 else w_in_rest[l - 1]
        gdn_small = w_in[:, RWKV_COLS + RET_COLS + GDN_MAIN_COLS:IN_COLS_FIRST]
        vres_cols = w_in[:, IN_COLS_FIRST:] if l > 0 else jnp.zeros((d, RWKV_VRES_RANK), F32)
        misc_w = jnp.concatenate([gdn_small, vres_cols], axis=1)
        misc_w = jnp.pad(misc_w, ((0, 0), (0, MISC_COLS - misc_w.shape[1])))
        w_packed = jnp.concatenate([w_in[:, :RWKV_COLS + RET_COLS + GDN_MAIN_COLS], misc_w], axis=1).astype(BF16)
        p_rwkv, p_ret, p_gdn, p_misc = _inproj(x, ng[2], w_packed)
        p_rwkv = p_rwkv.reshape(b, t, RWKV_COLS)
        p_ret = p_ret.reshape(b, t, RET_COLS)
        p_gdn = p_gdn.reshape(b, t, GDN_MAIN_COLS)
        p_misc = p_misc.reshape(b, t, MISC_COLS)

        rw = {"mu": rwkv_mu[l][None, :], "w0": rwkv_w0[l][None, :],
              "w_up": _pad_rows(rwkv_w_up[l], 0, LANES).astype(BF16), "a0": rwkv_a0[l][None, :],
              "a_up": _pad_rows(rwkv_a_up[l], RWKV_DECAY_RANK, LANES).astype(BF16),
              "g_up": rwkv_g_up[l].astype(BF16), "k_k": rwkv_k_k[l][None, :], "k_a": rwkv_k_a[l][None, :],
              "r_k": rwkv_r_k[l][None, :], "ln_w": rwkv_ln_w[l][None, :], "ln_b": rwkv_ln_b[l][None, :]}
        if l > 0:
            rw["mu_v"] = _pad_lanes(rwkv_mu_vres[l - 1], MISC_VRES, MISC_COLS)
            rw["v0"] = rwkv_v0[l - 1][None, :]
            rw["v_up"] = _pad_rows(rwkv_v_up[l - 1], MISC_VRES, MISC_COLS).astype(BF16)
        y_a, v_first = _rwkv(p_rwkv, p_misc, v_first, rw, consts)
        y_b = _ret(p_ret, cos, sin, ret_gn_w[l][None, :], consts)
        gd = {"conv_w": gdn_conv_w[l], "a_log": _pad_lanes(gdn_A_log[l], MISC_ALOG, MISC_COLS),
              "dt_bias": _pad_lanes(gdn_dt_bias[l], MISC_ALOG, MISC_COLS),
              "norm_w": jnp.tile(gdn_norm_w[l], H_GDN)[None, :]}
        y_c = _gdn(p_gdn, p_misc, gd, consts)

        mixer = (y_a.reshape(n, D_RWKV), y_b.reshape(n, D_RET), y_c.reshape(n, D_GDN), wo, ng[3])
        x = _ffn(x, ng[4], ng[5], wg, wu, wd, l, 1, mixer)
    return x.reshape(b, t, d)
```
